```python
import math
import jax, jax.numpy as jnp
from jax import lax
import numpy as np

D_MODEL = 1024
BATCH = 4
SEQ = 4096
DEPTH = 4

BRANCH_WIDTH = D_MODEL // 2
N_BRANCH = 3
S5_WIDTH = BRANCH_WIDTH
S5_GROUP = 16
S5_GROUPS = S5_WIDTH // S5_GROUP
S5_STATE = 64
S5_DT_MIN = 0.001
S5_DT_MAX = 0.1
LRU_WIDTH = BRANCH_WIDTH
LRU_HEADS = 8
LRU_HEAD_DIM = LRU_WIDTH // LRU_HEADS
LRU_C = 8.0
LRU_A_MIN = 0.9
LRU_A_MAX = 0.999
CONV_WIDTH = 4
FOX_HEAD_DIM = 64
FOX_HEADS = BRANCH_WIDTH // FOX_HEAD_DIM
FOX_WIDTH = FOX_HEADS * FOX_HEAD_DIM
FOX_FORGET_BIAS = 3.0
Q_BLOCK = 128
FFN_HIDDEN = ((8 * D_MODEL + 3 * 256 - 1) // (3 * 256)) * 256
ALPHA = (2.0 * DEPTH) ** 0.25
BETA = (8.0 * DEPTH) ** -0.25
LN_EPS = 1e-5
IN_SIZES = (S5_WIDTH, LRU_WIDTH, LRU_WIDTH, FOX_WIDTH, FOX_WIDTH, FOX_WIDTH, FOX_HEADS, N_BRANCH * D_MODEL)
IN_TOTAL = sum(IN_SIZES)

kernel_name = "hybrid_s5_rglru_fox_deepnorm"


def _layer_norm(x, g, b):
    x32 = x.astype(jnp.float32)
    mu = jnp.mean(x32, axis=-1, keepdims=True)
    var = jnp.mean(jnp.square(x32 - mu), axis=-1, keepdims=True)
    y = (x32 - mu) * lax.rsqrt(var + LN_EPS) * g.astype(jnp.float32) + b.astype(jnp.float32)
    return y.astype(x.dtype)


def _linear_scan(a, b):
    def combine(left, right):
        a_l, b_l = left
        a_r, b_r = right
        return a_r * a_l, a_r * b_l + b_r
    _, h = lax.associative_scan(combine, (a, b), axis=1)
    return h


def _s5_branch(u, a_re, a_im, log_dt, b_re, b_im, c_re, c_im, d_skip, w_glu, b_glu):
    bsz, L, _ = u.shape
    f32 = jnp.float32
    ug = u.astype(f32).reshape(bsz, L, S5_GROUPS, S5_GROUP)
    lam = lax.complex(a_re.astype(f32), a_im.astype(f32))
    dt = jnp.exp(log_dt.astype(f32))[:, None]
    lam_bar = jnp.exp(lam * dt)
    b_c = lax.complex(b_re.astype(f32), b_im.astype(f32))
    b_bar = ((lam_bar - 1.0) / lam)[:, :, None] * b_c
    bu = jnp.einsum('blgc,gpc->blgp', ug.astype(jnp.complex64), b_bar)
    h = _linear_scan(jnp.broadcast_to(lam_bar, bu.shape), bu)
    c_c = lax.complex(c_re.astype(f32), c_im.astype(f32))
    y = jnp.einsum('blgp,gcp->blgc', h, c_c).real + d_skip.astype(f32).reshape(S5_GROUPS, S5_GROUP) * ug
    y = jax.nn.gelu(y.reshape(bsz, L, S5_WIDTH))
    y = y * jax.nn.sigmoid(y @ w_glu.astype(f32) + b_glu.astype(f32))
    return y.astype(u.dtype)


def _rglru_branch(xb, gate, conv_w, conv_b, w_a, b_a, w_x, b_x, lam):
    bsz, L, _ = xb.shape
    f32 = jnp.float32
    xp = jnp.pad(xb, ((0, 0), (CONV_WIDTH - 1, 0), (0, 0)))
    xc = conv_b + sum(conv_w[k] * xp[:, k:k + L] for k in range(CONV_WIDTH))
    xh = xc.reshape(bsz, L, LRU_HEADS, LRU_HEAD_DIM)
    r = jax.nn.sigmoid(jnp.einsum('blhi,hij->blhj', xh, w_a) + b_a).astype(f32)
    i = jax.nn.sigmoid(jnp.einsum('blhi,hij->blhj', xh, w_x) + b_x).astype(f32)
    log_a = -LRU_C * jax.nn.softplus(-lam.astype(f32).reshape(LRU_HEADS, LRU_HEAD_DIM)) * r
    a = jnp.exp(log_a)
    mult = jnp.sqrt(-jnp.expm1(2.0 * log_a))
    h = _linear_scan(a, mult * (i * xh.astype(f32)))
    y = jax.nn.gelu(gate.astype(f32)) * h.reshape(bsz, L, LRU_WIDTH)
    return y.astype(xb.dtype)


def _fox_branch(q, k, v, fg_logit, b_f):
    bsz, L, _ = q.shape
    f32 = jnp.float32
    q = q.reshape(bsz, L, FOX_HEADS, FOX_HEAD_DIM)
    k = k.reshape(bsz, L, FOX_HEADS, FOX_HEAD_DIM)
    v = v.reshape(bsz, L, FOX_HEADS, FOX_HEAD_DIM)
    log_f = jax.nn.log_sigmoid((fg_logit + b_f).astype(f32))
    cum = jnp.cumsum(log_f, axis=1).transpose(0, 2, 1)
    kpos = jnp.arange(L)
    scale = FOX_HEAD_DIM ** -0.5

    def one_block(blk):
        start = blk * Q_BLOCK
        qb = lax.dynamic_slice_in_dim(q, start, Q_BLOCK, axis=1)
        cq = lax.dynamic_slice_in_dim(cum, start, Q_BLOCK, axis=2)
        s = jnp.einsum('bqhd,bkhd->bhqk', qb, k).astype(f32) * scale
        s = s + cq[..., None] - cum[:, :, None, :]
        qpos = start + jnp.arange(Q_BLOCK)
        s = jnp.where(kpos[None, :] <= qpos[:, None], s, -jnp.inf)
        p = jax.nn.softmax(s, axis=-1)
        return jnp.einsum('bhqk,bkhd->bqhd', p.astype(v.dtype), v)

    out = lax.map(one_block, jnp.arange(L // Q_BLOCK))
    return out.transpose(1, 0, 2, 3, 4).reshape(bsz, L, FOX_WIDTH)


def setup_inputs(seed: int = 0) -> dict:
    key = jax.random.key(seed)
    ks = iter(jax.random.split(key, 40))
    f32 = jnp.float32

    def nrm(shape, scale):
        return scale * jax.random.normal(next(ks), shape, f32)

    n = jnp.arange(S5_STATE, dtype=f32)
    x = nrm((BATCH, SEQ, D_MODEL), 1.0)
    w_in = nrm((DEPTH, D_MODEL, IN_TOTAL), D_MODEL ** -0.5)
    b_f = FOX_FORGET_BIAS + nrm((DEPTH, FOX_HEADS), 0.1)
    b_gate = nrm((DEPTH, N_BRANCH * D_MODEL), 0.01)
    s5_a_re = -0.5 + nrm((DEPTH, S5_GROUPS, S5_STATE), 0.01)
    s5_a_im = math.pi * n + nrm((DEPTH, S5_GROUPS, S5_STATE), 0.01)
    s5_log_dt = jax.random.uniform(next(ks), (DEPTH, S5_GROUPS), f32, math.log(S5_DT_MIN), math.log(S5_DT_MAX))
    s5_b_re = nrm((DEPTH, S5_GROUPS, S5_STATE, S5_GROUP), (2 * S5_GROUP) ** -0.5)
    s5_b_im = nrm((DEPTH, S5_GROUPS, S5_STATE, S5_GROUP), (2 * S5_GROUP) ** -0.5)
    s5_c_re = nrm((DEPTH, S5_GROUPS, S5_GROUP, S5_STATE), (2 * S5_STATE) ** -0.5)
    s5_c_im = nrm((DEPTH, S5_GROUPS, S5_GROUP, S5_STATE), (2 * S5_STATE) ** -0.5)
    s5_d = nrm((DEPTH, S5_WIDTH), 1.0)
    s5_w_glu = nrm((DEPTH, S5_WIDTH, S5_WIDTH), S5_WIDTH ** -0.5)
    s5_b_glu = nrm((DEPTH, S5_WIDTH), 0.01)
    lru_conv_w = nrm((DEPTH, CONV_WIDTH, LRU_WIDTH), CONV_WIDTH ** -0.5)
    lru_conv_b = nrm((DEPTH, LRU_WIDTH), 0.01)
    lru_w_a = nrm((DEPTH, LRU_HEADS, LRU_HEAD_DIM, LRU_HEAD_DIM), LRU_HEAD_DIM ** -0.5)
    lru_b_a = nrm((DEPTH, LRU_HEADS, LRU_HEAD_DIM), 0.01)
    lru_w_x = nrm((DEPTH, LRU_HEADS, LRU_HEAD_DIM, LRU_HEAD_DIM), LRU_HEAD_DIM ** -0.5)
    lru_b_x = nrm((DEPTH, LRU_HEADS, LRU_HEAD_DIM), 0.01)
    a_c = jax.random.uniform(next(ks), (DEPTH, LRU_WIDTH), f32, LRU_A_MIN, LRU_A_MAX)
    sig = a_c ** (1.0 / LRU_C)
    lru_lambda = jnp.log(sig) - jnp.log1p(-sig)
    w_branch = nrm((DEPTH, N_BRANCH, BRANCH_WIDTH, D_MODEL), BRANCH_WIDTH ** -0.5)
    w_out = nrm((DEPTH, D_MODEL, D_MODEL), BETA * D_MODEL ** -0.5)
    ln1_g = 1.0 + nrm((DEPTH, D_MODEL), 0.01)
    ln1_b = nrm((DEPTH, D_MODEL), 0.01)
    w_ffn_gate = nrm((DEPTH, D_MODEL, FFN_HIDDEN), D_MODEL ** -0.5)
    w_ffn_up = nrm((DEPTH, D_MODEL, FFN_HIDDEN), D_MODEL ** -0.5)
    w_ffn_down = nrm((DEPTH, FFN_HIDDEN, D_MODEL), BETA * FFN_HIDDEN ** -0.5)
    ln2_g = 1.0 + nrm((DEPTH, D_MODEL), 0.01)
    ln2_b = nrm((DEPTH, D_MODEL), 0.01)
    return {"x": x, "w_in": w_in, "b_f": b_f, "b_gate": b_gate,
            "s5_a_re": s5_a_re, "s5_a_im": s5_a_im, "s5_log_dt": s5_log_dt,
            "s5_b_re": s5_b_re, "s5_b_im": s5_b_im, "s5_c_re": s5_c_re, "s5_c_im": s5_c_im,
            "s5_d": s5_d, "s5_w_glu": s5_w_glu, "s5_b_glu": s5_b_glu,
            "lru_conv_w": lru_conv_w, "lru_conv_b": lru_conv_b,
            "lru_w_a": lru_w_a, "lru_b_a": lru_b_a, "lru_w_x": lru_w_x, "lru_b_x": lru_b_x,
            "lru_lambda": lru_lambda, "w_branch": w_branch, "w_out": w_out,
            "ln1_g": ln1_g, "ln1_b": ln1_b,
            "w_ffn_gate": w_ffn_gate, "w_ffn_up": w_ffn_up, "w_ffn_down": w_ffn_down,
            "ln2_g": ln2_g, "ln2_b": ln2_b}


def reference(x, w_in, b_f, b_gate, s5_a_re, s5_a_im, s5_log_dt, s5_b_re, s5_b_im, s5_c_re, s5_c_im,
              s5_d, s5_w_glu, s5_b_glu, lru_conv_w, lru_conv_b, lru_w_a, lru_b_a, lru_w_x, lru_b_x,
              lru_lambda, w_branch, w_out, ln1_g, ln1_b, w_ffn_gate, w_ffn_up, w_ffn_down, ln2_g, ln2_b):
    split_at = np.cumsum(IN_SIZES)[:-1].tolist()
    bsz, L, _ = x.shape
    for l in range(DEPTH):
        z = x @ w_in[l]
        u_s5, x_lru, g_lru, q, k, v, fg, gate_logits = jnp.split(z, split_at, axis=-1)
        y_s5 = _s5_branch(u_s5, s5_a_re[l], s5_a_im[l], s5_log_dt[l], s5_b_re[l], s5_b_im[l],
                          s5_c_re[l], s5_c_im[l], s5_d[l], s5_w_glu[l], s5_b_glu[l])
        y_lru = _rglru_branch(x_lru, g_lru, lru_conv_w[l], lru_conv_b[l], lru_w_a[l], lru_b_a[l],
                              lru_w_x[l], lru_b_x[l], lru_lambda[l])
        y_fox = _fox_branch(q, k, v, fg, b_f[l])
        ys = jnp.stack([y_s5, y_lru, y_fox], axis=2)
        proj = jnp.einsum('blkc,kcd->blkd', ys, w_branch[l])
        gates = jax.nn.sigmoid(gate_logits + b_gate[l]).reshape(bsz, L, N_BRANCH, D_MODEL)
        mixed = jnp.sum(gates * proj, axis=2) @ w_out[l]
        x = _layer_norm(ALPHA * x + mixed, ln1_g[l], ln1_b[l])
        hid = jax.nn.silu(x @ w_ffn_gate[l]) * (x @ w_ffn_up[l])
        x = _layer_norm(ALPHA * x + hid @ w_ffn_down[l], ln2_g[l], ln2_b[l])
    return x
```

```python
import functools
import math

import jax
import jax.numpy as jnp
from jax import lax
from jax.experimental import pallas as pl
from jax.experimental.pallas import tpu as pltpu

F32 = jnp.float32
BF16 = jnp.bfloat16

D_MODEL = 1024
DEPTH = 4
BRANCH_WIDTH = D_MODEL // 2
N_BRANCH = 3
S5_GROUP = 16
S5_GROUPS = BRANCH_WIDTH // S5_GROUP
S5_STATE = 64
LRU_HEADS = 8
LRU_HEAD_DIM = BRANCH_WIDTH // LRU_HEADS
LRU_C = 8.0
CONV_WIDTH = 4
FOX_HEAD_DIM = 64
FOX_HEADS = BRANCH_WIDTH // FOX_HEAD_DIM
FFN_HIDDEN = ((8 * D_MODEL + 3 * 256 - 1) // (3 * 256)) * 256
ALPHA = (2.0 * DEPTH) ** 0.25
LN_EPS = 1e-5

LANES = 128
SUBLANES = 8
S5_SLABS = 4
S5_SLAB_GROUPS = S5_GROUPS // S5_SLABS
S5_HALF = S5_SLAB_GROUPS * S5_STATE
S5_SLAB_IN = S5_SLAB_GROUPS * S5_GROUP
S5_NSTATE = S5_GROUPS * S5_STATE

ROW_TILE = 512
ATT_TILE = 256
FFN_CHUNK = 1408
VMEM_LIMIT = 56 * 1024 * 1024


def _params(n_axes):
    return pltpu.CompilerParams(
        dimension_semantics=("arbitrary",) * n_axes, vmem_limit_bytes=VMEM_LIMIT)


def _dot(a, b):
    return jnp.dot(a, b, preferred_element_type=F32)


def _sigmoid(x):
    return 1.0 / (1.0 + jnp.exp(-x))


def _gelu(x):
    return 0.5 * x * (1.0 + jnp.tanh(math.sqrt(2.0 / math.pi) * (x + 0.044715 * (x * x * x))))


def _layer_norm(r, g, b):
    mu = jnp.mean(r, axis=-1, keepdims=True)
    d = r - mu
    var = jnp.mean(d * d, axis=-1, keepdims=True)
    return d * lax.rsqrt(var + LN_EPS) * g + b


def _const_spec(shape):
    return pl.BlockSpec(shape, lambda b, i: (0,) * len(shape))


def _row_spec(width, tile=ROW_TILE):
    return pl.BlockSpec((1, tile, width), lambda b, i: (b, i, 0))


def _qkv_kernel(xb_ref, w_ref, bf_ref, tri_ref, q_ref, k_ref, v_ref, cum_ref, carry_ref):
    @pl.when(pl.program_id(1) == 0)
    def _():
        carry_ref[...] = jnp.zeros_like(carry_ref)

    tm = xb_ref.shape[1]
    z = _dot(xb_ref[0], w_ref[...])
    q_ref[0] = (z[:, 0:512] * (FOX_HEAD_DIM ** -0.5)).astype(BF16)
    k_ref[0] = z[:, 512:1024].astype(BF16)
    v_ref[0] = z[:, 1024:1536].astype(BF16)
    f = z[:, 1536:1536 + LANES] + bf_ref[...]
    lf = jnp.minimum(f, 0.0) - jnp.log1p(jnp.exp(-jnp.abs(f)))
    hi = lf.astype(BF16)
    r1 = lf - hi.astype(F32)
    mid = r1.astype(BF16)
    lo = (r1 - mid.astype(F32)).astype(BF16)
    tri = tri_ref[...]
    cum = _dot(tri, hi) + _dot(tri, mid) + _dot(tri, lo) + carry_ref[...]
    cum_ref[0] = cum
    carry_ref[...] = cum[tm - 1:tm, :]


def _qkv_call(xb, w, bf, tri):
    bsz, L, _ = xb.shape
    n = w.shape[1]
    out = [jax.ShapeDtypeStruct((bsz, L, BRANCH_WIDTH), BF16)] * 3 + [
        jax.ShapeDtypeStruct((bsz, L, LANES), F32)]
    return pl.pallas_call(
        _qkv_kernel,
        grid=(bsz, L // ROW_TILE),
        in_specs=[_row_spec(D_MODEL), _const_spec((D_MODEL, n)), _const_spec((1, LANES)),
                  _const_spec((ROW_TILE, ROW_TILE))],
        out_specs=[_row_spec(BRANCH_WIDTH)] * 3 + [_row_spec(LANES)],
        out_shape=out,
        scratch_shapes=[pltpu.VMEM((1, LANES), F32)],
        compiler_params=_params(2),
        name="qkv",
    )(xb, w, bf, tri)


def _s5_kernel(xb_ref, win_ref, wb_ref, wc_ref, lr_ref, li_ref, pr_ref, pi_ref, d_ref, wg_ref,
               bg_ref, o_ref, bu_scr, hb_scr, l_scr, e_scr, carry_scr):
    tc = xb_ref.shape[1]
    S = tc // SUBLANES
    H = S5_HALF

    @pl.when(pl.program_id(1) == 0)
    def _():
        carry_scr[...] = jnp.zeros_like(carry_scr)

    u = _dot(xb_ref[0], win_ref[...])
    ub = u.astype(BF16)
    nq = H // LANES
    for blk in range(S5_SLABS):
        bu = _dot(ub[:, blk * S5_SLAB_IN:(blk + 1) * S5_SLAB_IN], wb_ref[blk])
        for p in range(2 * nq):
            bu_scr[blk * 2 * nq + p] = bu[:, p * LANES:(p + 1) * LANES]

    for blk in range(S5_SLABS):
        cr = slice(blk * 2 * H, blk * 2 * H + H)
        ci = slice(blk * 2 * H + H, (blk + 1) * 2 * H)
        cl = slice(blk * H, (blk + 1) * H)
        lr = [lr_ref[:, blk * H + q * LANES:blk * H + (q + 1) * LANES] for q in range(nq)]
        li = [li_ref[:, blk * H + q * LANES:blk * H + (q + 1) * LANES] for q in range(nq)]
        hr = [jnp.zeros((SUBLANES, LANES), F32)] * nq
        hi = [jnp.zeros((SUBLANES, LANES), F32)] * nq
        for t in range(S):
            rows = pl.ds(t, SUBLANES, stride=S)
            for q in range(nq):
                pr_i = blk * 2 * nq + q
                pi_i = pr_i + nq
                nhr = lr[q] * hr[q] - li[q] * hi[q] + bu_scr[pr_i, rows, :]
                nhi = lr[q] * hi[q] + li[q] * hr[q] + bu_scr[pi_i, rows, :]
                hr[q], hi[q] = nhr, nhi
                bu_scr[pr_i, rows, :] = nhr
                bu_scr[pi_i, rows, :] = nhi
        for q in range(nq):
            l_scr[:, q * LANES:(q + 1) * LANES] = hr[q]
            l_scr[:, H + q * LANES:H + (q + 1) * LANES] = hi[q]
        psr = pr_ref[S - 1:S, cl]
        psi = pi_ref[S - 1:S, cl]
        er = carry_scr[:, cr]
        ei = carry_scr[:, ci]
        for c in range(SUBLANES):
            e_scr[c:c + 1, 0:H] = er
            e_scr[c:c + 1, H:2 * H] = ei
            ner = l_scr[c:c + 1, 0:H] + psr * er - psi * ei
            nei = l_scr[c:c + 1, H:2 * H] + psr * ei + psi * er
            er, ei = ner, nei
        carry_scr[:, cr] = er
        carry_scr[:, ci] = ei
        rb = 2 * SUBLANES
        for c in range(SUBLANES):
            ebr = jnp.broadcast_to(e_scr[c:c + 1, 0:H], (rb, H))
            ebi = jnp.broadcast_to(e_scr[c:c + 1, H:2 * H], (rb, H))
            for tb in range(0, S, rb):
                r0 = c * S + tb
                pr = pr_ref[tb:tb + rb, cl]
                pi = pi_ref[tb:tb + rb, cl]
                bur = jnp.concatenate(
                    [bu_scr[blk * 2 * nq + q, r0:r0 + rb, :] for q in range(nq)], axis=1)
                bui = jnp.concatenate(
                    [bu_scr[blk * 2 * nq + nq + q, r0:r0 + rb, :] for q in range(nq)], axis=1)
                hb_scr[r0:r0 + rb, cr] = (bur + pr * ebr - pi * ebi).astype(BF16)
                hb_scr[r0:r0 + rb, ci] = (bui + pr * ebi + pi * ebr).astype(BF16)

    ys = [_dot(hb_scr[:, blk * 2 * H:(blk + 1) * 2 * H], wc_ref[blk]) for blk in range(S5_SLABS)]
    y = jnp.concatenate(ys, axis=1) + d_ref[...] * u
    y = _gelu(y)
    y = y * _sigmoid(_dot(y.astype(BF16), wg_ref[...]) + bg_ref[...])
    o_ref[0] = y.astype(BF16)


def _s5_call(xb, win, wb, wc, lr, li, pr, pi, d, wg, bg):
    bsz, L, _ = xb.shape
    tc = ROW_TILE
    S = tc // SUBLANES
    ncol = 2 * S5_NSTATE
    return pl.pallas_call(
        _s5_kernel,
        grid=(bsz, L // tc),
        in_specs=[_row_spec(D_MODEL), _const_spec((D_MODEL, BRANCH_WIDTH)),
                  _const_spec((S5_SLABS, S5_SLAB_IN, 2 * S5_HALF)),
                  _const_spec((S5_SLABS, 2 * S5_HALF, S5_SLAB_IN)),
                  _const_spec((SUBLANES, S5_NSTATE)), _const_spec((SUBLANES, S5_NSTATE)),
                  _const_spec((S, S5_NSTATE)), _const_spec((S, S5_NSTATE)),
                  _const_spec((1, BRANCH_WIDTH)), _const_spec((BRANCH_WIDTH, BRANCH_WIDTH)),
                  _const_spec((1, BRANCH_WIDTH))],
        out_specs=_row_spec(BRANCH_WIDTH),
        out_shape=jax.ShapeDtypeStruct((bsz, L, BRANCH_WIDTH), BF16),
        scratch_shapes=[pltpu.VMEM((ncol // LANES, tc, LANES), F32), pltpu.VMEM((tc, ncol), BF16),
                        pltpu.VMEM((SUBLANES, 2 * S5_HALF), F32),
                        pltpu.VMEM((SUBLANES, 2 * S5_HALF), F32),
                        pltpu.VMEM((1, ncol), F32)],
        compiler_params=_params(2),
        name="s5",
    )(xb, win, wb, wc, lr, li, pr, pi, d, wg, bg)


def _lru_kernel(xb_ref, win_ref, cw_ref, cb_ref, wax_ref, bax_ref, clam_ref, o_ref,
                xl_scr, a_scr, b_scr, g_scr, l_scr, e_scr, carry_scr):
    tc = xb_ref.shape[1]
    S = tc // SUBLANES
    W = BRANCH_WIDTH
    halo = SUBLANES

    @pl.when(pl.program_id(1) == 0)
    def _():
        xl_scr[0:halo, :] = jnp.zeros((halo, W), F32)
        carry_scr[...] = jnp.zeros_like(carry_scr)

    z = _dot(xb_ref[0], win_ref[...])
    xl = z[:, 0:W]
    g_scr[...] = z[:, W:2 * W]
    xl_scr[halo:halo + tc, :] = xl
    xc = cb_ref[...] + cw_ref[CONV_WIDTH - 1:CONV_WIDTH, :] * xl
    for k in range(1, CONV_WIDTH):
        xc = xc + cw_ref[CONV_WIDTH - 1 - k:CONV_WIDTH - k, :] * xl_scr[halo - k:halo - k + tc, :]
    xl_scr[0:halo, :] = xl_scr[tc:tc + halo, :]

    ra = _dot(xc.astype(BF16), wax_ref[...]) + bax_ref[...]
    r = _sigmoid(ra[:, 0:W])
    ig = _sigmoid(ra[:, W:2 * W])
    log_a = clam_ref[...] * r
    a = jnp.exp(log_a)
    mult = jnp.sqrt(-jnp.tanh(log_a) * (a * a + 1.0))
    b = mult * (ig * xc)
    nq = W // LANES
    for q in range(nq):
        a_scr[q] = a[:, q * LANES:(q + 1) * LANES]
        b_scr[q] = b[:, q * LANES:(q + 1) * LANES]

    h = [jnp.zeros((SUBLANES, LANES), F32)] * nq
    ap = [jnp.ones((SUBLANES, LANES), F32)] * nq
    for t in range(S):
        rows = pl.ds(t, SUBLANES, stride=S)
        for q in range(nq):
            at = a_scr[q, rows, :]
            h[q] = at * h[q] + b_scr[q, rows, :]
            ap[q] = at * ap[q]
            b_scr[q, rows, :] = h[q]
            a_scr[q, rows, :] = ap[q]
    for q in range(nq):
        l_scr[0:SUBLANES, q * LANES:(q + 1) * LANES] = h[q]
        l_scr[SUBLANES:2 * SUBLANES, q * LANES:(q + 1) * LANES] = ap[q]
    e = carry_scr[...]
    for c in range(SUBLANES):
        e_scr[c:c + 1, :] = e
        e = l_scr[c:c + 1, :] + l_scr[SUBLANES + c:SUBLANES + c + 1, :] * e
    carry_scr[...] = e
    for c in range(SUBLANES):
        rs = slice(c * S, (c + 1) * S)
        hl = jnp.concatenate([b_scr[q, rs, :] for q in range(nq)], axis=1)
        al = jnp.concatenate([a_scr[q, rs, :] for q in range(nq)], axis=1)
        hs = hl + al * e_scr[c:c + 1, :]
        o_ref[0, rs, :] = (_gelu(g_scr[rs, :]) * hs).astype(BF16)


def _lru_call(xb, win, cw, cb, wax, bax, clam):
    bsz, L, _ = xb.shape
    tc = ROW_TILE
    W = BRANCH_WIDTH
    return pl.pallas_call(
        _lru_kernel,
        grid=(bsz, L // tc),
        in_specs=[_row_spec(D_MODEL), _const_spec((D_MODEL, 2 * W)), _const_spec((CONV_WIDTH, W)),
                  _const_spec((1, W)), _const_spec((W, 2 * W)), _const_spec((1, 2 * W)),
                  _const_spec((1, W))],
        out_specs=_row_spec(W),
        out_shape=jax.ShapeDtypeStruct((bsz, L, W), BF16),
        scratch_shapes=[pltpu.VMEM((tc + 2 * SUBLANES, W), F32),
                        pltpu.VMEM((W // LANES, tc, LANES), F32),
                        pltpu.VMEM((W // LANES, tc, LANES), F32), pltpu.VMEM((tc, W), F32),
                        pltpu.VMEM((2 * SUBLANES, W), F32), pltpu.VMEM((SUBLANES, W), F32),
                        pltpu.VMEM((1, W), F32)],
        compiler_params=_params(2),
        name="lru",
    )(xb, win, cw, cb, wax, bax, clam)


def _fox_kernel(q_ref, k_ref, v_ref, cum_ref, cumt_ref, o_ref):
    T = q_ref.shape[1]
    i = pl.program_id(1)
    dh = FOX_HEAD_DIM
    nt = (((1,), (1,)), ((), ()))
    row = lax.broadcasted_iota(jnp.int32, (T, T), 0)
    col = lax.broadcasted_iota(jnp.int32, (T, T), 1)
    causal = col <= row

    for h in range(FOX_HEADS):
        hs = slice(h * dh, (h + 1) * dh)
        qh = q_ref[0, :, hs]
        cq = cum_ref[0, :, h:h + 1]

        def scores(j):
            r0 = pl.multiple_of(j * T, T)
            kj = k_ref[0, pl.ds(r0, T), hs]
            ck = cumt_ref[0, h:h + 1, pl.ds(r0, T)]
            s = lax.dot_general(qh, kj, nt, preferred_element_type=F32)
            return s + (cq - ck), r0

        def update(carry, s, r0):
            m, l, acc = carry
            m_new = jnp.maximum(m, jnp.max(s, axis=-1, keepdims=True))
            alpha = jnp.exp(m - m_new)
            p = jnp.exp(s - m_new)
            l = alpha * l + jnp.sum(p, axis=-1, keepdims=True)
            acc = alpha * acc + _dot(p.astype(BF16), v_ref[0, pl.ds(r0, T), hs])
            return m_new, l, acc

        def body(j, carry):
            s, r0 = scores(j)
            return update(carry, s, r0)

        init = (jnp.full((T, 1), -jnp.inf, F32), jnp.zeros((T, 1), F32), jnp.zeros((T, dh), F32))
        carry = lax.fori_loop(0, i, body, init)
        s, r0 = scores(i)
        _, l, acc = update(carry, jnp.where(causal, s, -jnp.inf), r0)
        o_ref[0, :, hs] = (acc / l).astype(BF16)


def _fox_call(q, k, v, cum, cumt):
    bsz, L, W = q.shape
    T = ATT_TILE
    return pl.pallas_call(
        _fox_kernel,
        grid=(bsz, L // T),
        in_specs=[_row_spec(W, T),
                  pl.BlockSpec((1, L, W), lambda b, i: (b, 0, 0)),
                  pl.BlockSpec((1, L, W), lambda b, i: (b, 0, 0)),
                  _row_spec(LANES, T),
                  pl.BlockSpec((1, SUBLANES, L), lambda b, i: (b, 0, 0))],
        out_specs=_row_spec(W, T),
        out_shape=jax.ShapeDtypeStruct((bsz, L, W), BF16),
        compiler_params=_params(2),
        name="fox",
    )(q, k, v, cum, cumt)


def _merge_kernel(x_ref, xb_ref, y1_ref, y2_ref, y3_ref, wg_ref, bg_ref, wbr_ref, wo_ref,
                  g_ref, b_ref, o_ref, ob_ref):
    xb = xb_ref[0]
    mixed = None
    for k, y_ref in enumerate((y1_ref, y2_ref, y3_ref)):
        cs = slice(k * D_MODEL, (k + 1) * D_MODEL)
        gate = _sigmoid(_dot(xb, wg_ref[:, cs]) + bg_ref[:, cs])
        term = gate * _dot(y_ref[0], wbr_ref[k])
        mixed = term if mixed is None else mixed + term
    r = ALPHA * x_ref[0] + _dot(mixed.astype(BF16), wo_ref[...])
    y = _layer_norm(r, g_ref[...], b_ref[...])
    o_ref[0] = y
    ob_ref[0] = y.astype(BF16)


def _merge_call(x, xb, y1, y2, y3, wg, bg, wbr, wo, g, b):
    bsz, L, D = x.shape
    W = BRANCH_WIDTH
    return pl.pallas_call(
        _merge_kernel,
        grid=(bsz, L // ROW_TILE),
        in_specs=[_row_spec(D), _row_spec(D), _row_spec(W), _row_spec(W), _row_spec(W),
                  _const_spec((D, N_BRANCH * D)), _const_spec((1, N_BRANCH * D)),
                  _const_spec((N_BRANCH, W, D)), _const_spec((D, D)),
                  _const_spec((1, D)), _const_spec((1, D))],
        out_specs=[_row_spec(D), _row_spec(D)],
        out_shape=[jax.ShapeDtypeStruct((bsz, L, D), F32), jax.ShapeDtypeStruct((bsz, L, D), BF16)],
        compiler_params=_params(2),
        name="merge",
    )(x, xb, y1, y2, y3, wg, bg, wbr, wo, g, b)


def _ffn_kernel(x_ref, xb_ref, wg_ref, wu_ref, wd_ref, g_ref, b_ref, o_ref, ob_ref):
    xb = xb_ref[0]
    acc = None
    for c in range(FFN_HIDDEN // FFN_CHUNK):
        cs = slice(c * FFN_CHUNK, (c + 1) * FFN_CHUNK)
        gt = _dot(xb, wg_ref[:, cs])
        hid = (gt * _sigmoid(gt)) * _dot(xb, wu_ref[:, cs])
        term = _dot(hid.astype(BF16), wd_ref[cs, :])
        acc = term if acc is None else acc + term
    y = _layer_norm(ALPHA * x_ref[0] + acc, g_ref[...], b_ref[...])
    o_ref[0] = y
    ob_ref[0] = y.astype(BF16)


def _ffn_call(x, xb, wg, wu, wd, g, b):
    bsz, L, D = x.shape
    Hd = FFN_HIDDEN
    once = pl.Buffered(1)
    return pl.pallas_call(
        _ffn_kernel,
        grid=(bsz, L // ROW_TILE),
        in_specs=[_row_spec(D), _row_spec(D),
                  pl.BlockSpec((D, Hd), lambda b, i: (0, 0), pipeline_mode=once),
                  pl.BlockSpec((D, Hd), lambda b, i: (0, 0), pipeline_mode=once),
                  pl.BlockSpec((Hd, D), lambda b, i: (0, 0), pipeline_mode=once),
                  _const_spec((1, D)), _const_spec((1, D))],
        out_specs=[_row_spec(D), _row_spec(D)],
        out_shape=[jax.ShapeDtypeStruct((bsz, L, D), F32), jax.ShapeDtypeStruct((bsz, L, D), BF16)],
        compiler_params=_params(2),
        name="ffn",
    )(x, xb, wg, wu, wd, g, b)


def _s5_weights(a_re, a_im, log_dt, b_re, b_im, c_re, c_im, sub_len):
    lam = lax.complex(a_re, a_im)
    dt = jnp.exp(log_dt)[:, None]
    lam_bar = jnp.exp(lam * dt)
    b_bar = ((lam_bar - 1.0) / lam)[:, :, None] * lax.complex(b_re, b_im)
    c_c = lax.complex(c_re, c_im)
    eye = jnp.eye(S5_SLAB_GROUPS, dtype=F32)
    bb = b_bar.reshape(S5_SLABS, S5_SLAB_GROUPS, S5_STATE, S5_GROUP)
    cc = c_c.reshape(S5_SLABS, S5_SLAB_GROUPS, S5_GROUP, S5_STATE)

    def in_w(part):
        return jnp.einsum('bgpc,gh->bgchp', part, eye).reshape(S5_SLABS, S5_SLAB_IN, S5_HALF)

    def out_w(part):
        return jnp.einsum('bgcp,gh->bhpgc', part, eye).reshape(S5_SLABS, S5_HALF, S5_SLAB_IN)

    wb = jnp.concatenate([in_w(jnp.real(bb)), in_w(jnp.imag(bb))], axis=2).astype(BF16)
    wc = jnp.concatenate([out_w(jnp.real(cc)), -out_w(jnp.imag(cc))], axis=1).astype(BF16)
    pw = lax.associative_scan(jnp.multiply, jnp.broadcast_to(lam_bar, (sub_len,) + lam_bar.shape),
                              axis=0)
    pr = jnp.real(pw).reshape(sub_len, S5_NSTATE)
    pi = jnp.imag(pw).reshape(sub_len, S5_NSTATE)
    lr = jnp.broadcast_to(jnp.real(lam_bar).reshape(1, S5_NSTATE), (SUBLANES, S5_NSTATE))
    li = jnp.broadcast_to(jnp.imag(lam_bar).reshape(1, S5_NSTATE), (SUBLANES, S5_NSTATE))
    return wb, wc, lr, li, pr, pi


def _block_diag(w):
    n, d, _ = w.shape
    return jnp.einsum('hij,hg->higj', w, jnp.eye(n, dtype=w.dtype)).reshape(n * d, n * d)


def kernel(x, w_in, b_f, b_gate, s5_a_re, s5_a_im, s5_log_dt, s5_b_re, s5_b_im, s5_c_re, s5_c_im,
           s5_d, s5_w_glu, s5_b_glu, lru_conv_w, lru_conv_b, lru_w_a, lru_b_a, lru_w_x, lru_b_x,
           lru_lambda, w_branch, w_out, ln1_g, ln1_b, w_ffn_gate, w_ffn_up, w_ffn_down, ln2_g, ln2_b):
    bsz, L, D = x.shape
    W = BRANCH_WIDTH
    sub_len = ROW_TILE // SUBLANES
    tri = (lax.broadcasted_iota(jnp.int32, (ROW_TILE, ROW_TILE), 1)
           <= lax.broadcasted_iota(jnp.int32, (ROW_TILE, ROW_TILE), 0)).astype(BF16)
    o_s5, o_lx, o_q, o_f, o_g = 0, W, 3 * W, 6 * W, 6 * W + FOX_HEADS
    xb = x.astype(BF16)
    for l in range(DEPTH):
        wl = w_in[l]
        w_s5 = wl[:, o_s5:o_lx].astype(BF16)
        w_lru = wl[:, o_lx:o_q].astype(BF16)
        w_qkv = jnp.concatenate(
            [wl[:, o_q:o_f], jnp.pad(wl[:, o_f:o_g], ((0, 0), (0, LANES - FOX_HEADS)))],
            axis=1).astype(BF16)
        w_gate = wl[:, o_g:].astype(BF16)
        bf = jnp.pad(b_f[l], (0, LANES - FOX_HEADS)).reshape(1, LANES)

        q, k, v, cum = _qkv_call(xb, w_qkv, bf, tri)
        cumt = jnp.transpose(cum[:, :, 0:SUBLANES], (0, 2, 1))

        wb, wc, lr, li, pr, pi = _s5_weights(s5_a_re[l], s5_a_im[l], s5_log_dt[l], s5_b_re[l],
                                             s5_b_im[l], s5_c_re[l], s5_c_im[l], sub_len)
        y_s5 = _s5_call(xb, w_s5, wb, wc, lr, li, pr, pi, s5_d[l].reshape(1, W),
                        s5_w_glu[l].astype(BF16), s5_b_glu[l].reshape(1, W))

        wax = jnp.concatenate([_block_diag(lru_w_a[l]), _block_diag(lru_w_x[l])], axis=1).astype(BF16)
        bax = jnp.concatenate([lru_b_a[l].reshape(1, W), lru_b_x[l].reshape(1, W)], axis=1)
        clam = (-LRU_C * jax.nn.softplus(-lru_lambda[l])).reshape(1, W)
        y_lru = _lru_call(xb, w_lru, lru_conv_w[l], lru_conv_b[l].reshape(1, W), wax, bax, clam)

        y_fox = _fox_call(q, k, v, cum, cumt)

        x, xb = _merge_call(x, xb, y_s5, y_lru, y_fox, w_gate, b_gate[l].reshape(1, N_BRANCH * D),
                            w_branch[l].astype(BF16), w_out[l].astype(BF16),
                            ln1_g[l].reshape(1, D), ln1_b[l].reshape(1, D))
        x, xb = _ffn_call(x, xb, w_ffn_gate[l].astype(BF16), w_ffn_up[l].astype(BF16),
                          w_ffn_down[l].astype(BF16), ln2_g[l].reshape(1, D), ln2_b[l].reshape(1, D))
    return x
```

```python
import functools
import math

import jax
import jax.numpy as jnp
import numpy as np
from jax import lax
from jax.experimental import pallas as pl
from jax.experimental.pallas import tpu as pltpu

F32 = jnp.float32
BF16 = jnp.bfloat16

D_MODEL = 1024
DEPTH = 4
BRANCH_WIDTH = D_MODEL // 2
N_BRANCH = 3
S5_GROUP = 16
S5_GROUPS = BRANCH_WIDTH // S5_GROUP
S5_STATE = 64
LRU_HEADS = 8
LRU_HEAD_DIM = BRANCH_WIDTH // LRU_HEADS
LRU_C = 8.0
CONV_WIDTH = 4
FOX_HEAD_DIM = 64
FOX_HEADS = BRANCH_WIDTH // FOX_HEAD_DIM
FOX_ACC_ROWS = FOX_HEAD_DIM + 16
LOG2E = math.log2(math.e)
FFN_HIDDEN = ((8 * D_MODEL + 3 * 256 - 1) // (3 * 256)) * 256
ALPHA = (2.0 * DEPTH) ** 0.25
LN_EPS = 1e-5

LANES = 128
SUBLANES = 8
S5_SLABS = 4
S5_SLAB_GROUPS = S5_GROUPS // S5_SLABS
S5_HALF = S5_SLAB_GROUPS * S5_STATE
S5_SLAB_IN = S5_SLAB_GROUPS * S5_GROUP
S5_NSTATE = S5_GROUPS * S5_STATE

ROW_TILE = 512
ATT_TILE = 256
FFN_CHUNK = 1408
VMEM_LIMIT = 56 * 1024 * 1024


def _params(n_axes):
    return pltpu.CompilerParams(
        dimension_semantics=("arbitrary",) * n_axes, vmem_limit_bytes=VMEM_LIMIT)


def _dot(a, b):
    return jnp.dot(a, b, preferred_element_type=F32)


def _sigmoid(x):
    return 1.0 / (1.0 + jnp.exp(-x))


def _gelu(x):
    return 0.5 * x * (1.0 + jnp.tanh(math.sqrt(2.0 / math.pi) * (x + 0.044715 * (x * x * x))))


def _layer_norm(r, g, b):
    mu = jnp.mean(r, axis=-1, keepdims=True)
    d = r - mu
    var = jnp.mean(d * d, axis=-1, keepdims=True)
    return d * lax.rsqrt(var + LN_EPS) * g + b


def _const_spec(shape):
    return pl.BlockSpec(shape, lambda b, i: (0,) * len(shape))


def _row_spec(width, tile=ROW_TILE):
    return pl.BlockSpec((1, tile, width), lambda b, i: (b, i, 0))


def _split3(x):
    hi = x.astype(BF16)
    r1 = x - hi.astype(F32)
    mid = r1.astype(BF16)
    lo = (r1 - mid.astype(F32)).astype(BF16)
    return hi, mid, lo


def _qkv_kernel(xb_ref, wqk_ref, wvt_ref, wf_ref, bf_ref, tri_ref, selq_ref, selk_ref, oneq_ref,
                onek_ref, qa_ref, ka_ref, vt_ref, carry_ref):
    @pl.when(pl.program_id(1) == 0)
    def _():
        carry_ref[...] = jnp.zeros_like(carry_ref)

    tm = xb_ref.shape[1]
    x = xb_ref[0]
    f = _dot(x, wf_ref[...]) + bf_ref[...]
    lf = jnp.minimum(f, 0.0) - jnp.log1p(jnp.exp(-jnp.abs(f)))
    hi, mid, lo = _split3(lf)
    tri = tri_ref[...]
    cum = _dot(tri, hi) + _dot(tri, mid) + _dot(tri, lo) + carry_ref[...]
    carry_ref[...] = cum[tm - 1:tm, :]

    c_hi, c_mid, c_lo = _split3(cum * LOG2E)
    lane = lax.broadcasted_iota(jnp.int32, cum.shape, 1)
    h8 = FOX_HEADS
    c3 = jnp.where(lane < h8, c_hi.astype(F32),
                   jnp.where(lane < 2 * h8, pltpu.roll(c_mid.astype(F32), h8, 1),
                             jnp.where(lane < 3 * h8, pltpu.roll(c_lo.astype(F32), 2 * h8, 1), 0.0)))
    c3 = c3.astype(BF16)
    z = _dot(x, wqk_ref[...])
    n = FOX_HEADS * LANES
    qa_ref[0] = (z[:, 0:n] + _dot(c3, selq_ref[...]) + oneq_ref[...]).astype(BF16)
    ka_ref[0] = (z[:, n:2 * n] + _dot(c3, selk_ref[...]) + onek_ref[...]).astype(BF16)
    nt = (((1,), (1,)), ((), ()))
    vt_ref[0] = lax.dot_general(wvt_ref[...], x, nt, preferred_element_type=F32).astype(BF16)


def _qkv_call(xb, wqk, wvt, wf, bf, tri, selq, selk, oneq, onek):
    bsz, L, D = xb.shape
    n = FOX_HEADS * LANES
    W = BRANCH_WIDTH
    return pl.pallas_call(
        _qkv_kernel,
        grid=(bsz, L // ROW_TILE),
        in_specs=[_row_spec(D), _const_spec((D, 2 * n)), _const_spec((W, D)),
                  _const_spec((D, LANES)), _const_spec((1, LANES)),
                  _const_spec((ROW_TILE, ROW_TILE)), _const_spec((LANES, n)),
                  _const_spec((LANES, n)), _const_spec((1, n)), _const_spec((1, n))],
        out_specs=[_row_spec(n), _row_spec(n),
                   pl.BlockSpec((1, W, ROW_TILE), lambda b, i: (b, 0, i))],
        out_shape=[jax.ShapeDtypeStruct((bsz, L, n), BF16), jax.ShapeDtypeStruct((bsz, L, n), BF16),
                   jax.ShapeDtypeStruct((bsz, W, L), BF16)],
        scratch_shapes=[pltpu.VMEM((1, LANES), F32)],
        compiler_params=_params(2),
        name="qkv",
    )(xb, wqk, wvt, wf, bf, tri, selq, selk, oneq, onek)


def _s5_kernel(xb_ref, win_ref, wb_ref, wc_ref, lr_ref, li_ref, pr_ref, pi_ref, d_ref, wg_ref,
               bg_ref, o_ref, bu_scr, hb_scr, l_scr, e_scr, carry_scr):
    tc = xb_ref.shape[1]
    S = tc // SUBLANES
    H = S5_HALF

    @pl.when(pl.program_id(1) == 0)
    def _():
        carry_scr[...] = jnp.zeros_like(carry_scr)

    u = _dot(xb_ref[0], win_ref[...])
    ub = u.astype(BF16)
    nq = H // LANES
    for blk in range(S5_SLABS):
        bu = _dot(ub[:, blk * S5_SLAB_IN:(blk + 1) * S5_SLAB_IN], wb_ref[blk])
        for p in range(2 * nq):
            bu_scr[blk * 2 * nq + p] = bu[:, p * LANES:(p + 1) * LANES]

    for blk in range(S5_SLABS):
        cr = slice(blk * 2 * H, blk * 2 * H + H)
        ci = slice(blk * 2 * H + H, (blk + 1) * 2 * H)
        cl = slice(blk * H, (blk + 1) * H)
        lr = [lr_ref[:, blk * H + q * LANES:blk * H + (q + 1) * LANES] for q in range(nq)]
        li = [li_ref[:, blk * H + q * LANES:blk * H + (q + 1) * LANES] for q in range(nq)]
        hr = [jnp.zeros((SUBLANES, LANES), F32)] * nq
        hi = [jnp.zeros((SUBLANES, LANES), F32)] * nq
        for t in range(S):
            rows = pl.ds(t, SUBLANES, stride=S)
            for q in range(nq):
                pr_i = blk * 2 * nq + q
                pi_i = pr_i + nq
                nhr = lr[q] * hr[q] - li[q] * hi[q] + bu_scr[pr_i, rows, :]
                nhi = lr[q] * hi[q] + li[q] * hr[q] + bu_scr[pi_i, rows, :]
                hr[q], hi[q] = nhr, nhi
                bu_scr[pr_i, rows, :] = nhr
                bu_scr[pi_i, rows, :] = nhi
        for q in range(nq):
            l_scr[:, q * LANES:(q + 1) * LANES] = hr[q]
            l_scr[:, H + q * LANES:H + (q + 1) * LANES] = hi[q]
        psr = pr_ref[S - 1:S, cl]
        psi = pi_ref[S - 1:S, cl]
        er = carry_scr[:, cr]
        ei = carry_scr[:, ci]
        for c in range(SUBLANES):
            e_scr[c:c + 1, 0:H] = er
            e_scr[c:c + 1, H:2 * H] = ei
            ner = l_scr[c:c + 1, 0:H] + psr * er - psi * ei
            nei = l_scr[c:c + 1, H:2 * H] + psr * ei + psi * er
            er, ei = ner, nei
        carry_scr[:, cr] = er
        carry_scr[:, ci] = ei
        rb = 2 * SUBLANES
        for c in range(SUBLANES):
            ebr = jnp.broadcast_to(e_scr[c:c + 1, 0:H], (rb, H))
            ebi = jnp.broadcast_to(e_scr[c:c + 1, H:2 * H], (rb, H))
            for tb in range(0, S, rb):
                r0 = c * S + tb
                pr = pr_ref[tb:tb + rb, cl]
                pi = pi_ref[tb:tb + rb, cl]
                bur = jnp.concatenate(
                    [bu_scr[blk * 2 * nq + q, r0:r0 + rb, :] for q in range(nq)], axis=1)
                bui = jnp.concatenate(
                    [bu_scr[blk * 2 * nq + nq + q, r0:r0 + rb, :] for q in range(nq)], axis=1)
                hb_scr[r0:r0 + rb, cr] = (bur + pr * ebr - pi * ebi).astype(BF16)
                hb_scr[r0:r0 + rb, ci] = (bui + pr * ebi + pi * ebr).astype(BF16)

    ys = [_dot(hb_scr[:, blk * 2 * H:(blk + 1) * 2 * H], wc_ref[blk]) for blk in range(S5_SLABS)]
    y = jnp.concatenate(ys, axis=1) + d_ref[...] * u
    y = _gelu(y)
    y = y * _sigmoid(_dot(y.astype(BF16), wg_ref[...]) + bg_ref[...])
    o_ref[0] = y.astype(BF16)


def _s5_call(xb, win, wb, wc, lr, li, pr, pi, d, wg, bg):
    bsz, L, _ = xb.shape
    tc = ROW_TILE
    S = tc // SUBLANES
    ncol = 2 * S5_NSTATE
    return pl.pallas_call(
        _s5_kernel,
        grid=(bsz, L // tc),
        in_specs=[_row_spec(D_MODEL), _const_spec((D_MODEL, BRANCH_WIDTH)),
                  _const_spec((S5_SLABS, S5_SLAB_IN, 2 * S5_HALF)),
                  _const_spec((S5_SLABS, 2 * S5_HALF, S5_SLAB_IN)),
                  _const_spec((SUBLANES, S5_NSTATE)), _const_spec((SUBLANES, S5_NSTATE)),
                  _const_spec((S, S5_NSTATE)), _const_spec((S, S5_NSTATE)),
                  _const_spec((1, BRANCH_WIDTH)), _const_spec((BRANCH_WIDTH, BRANCH_WIDTH)),
                  _const_spec((1, BRANCH_WIDTH))],
        out_specs=_row_spec(BRANCH_WIDTH),
        out_shape=jax.ShapeDtypeStruct((bsz, L, BRANCH_WIDTH), BF16),
        scratch_shapes=[pltpu.VMEM((ncol // LANES, tc, LANES), F32), pltpu.VMEM((tc, ncol), BF16),
                        pltpu.VMEM((SUBLANES, 2 * S5_HALF), F32),
                        pltpu.VMEM((SUBLANES, 2 * S5_HALF), F32),
                        pltpu.VMEM((1, ncol), F32)],
        compiler_params=_params(2),
        name="s5",
    )(xb, win, wb, wc, lr, li, pr, pi, d, wg, bg)


def _lru_kernel(xb_ref, win_ref, cw_ref, cb_ref, wax_ref, bax_ref, clam_ref, o_ref,
                xl_scr, a_scr, b_scr, g_scr, l_scr, e_scr, carry_scr):
    tc = xb_ref.shape[1]
    S = tc // SUBLANES
    W = BRANCH_WIDTH
    halo = SUBLANES

    @pl.when(pl.program_id(1) == 0)
    def _():
        xl_scr[0:halo, :] = jnp.zeros((halo, W), F32)
        carry_scr[...] = jnp.zeros_like(carry_scr)

    z = _dot(xb_ref[0], win_ref[...])
    xl = z[:, 0:W]
    g_scr[...] = z[:, W:2 * W]
    xl_scr[halo:halo + tc, :] = xl
    xc = cb_ref[...] + cw_ref[CONV_WIDTH - 1:CONV_WIDTH, :] * xl
    for k in range(1, CONV_WIDTH):
        xc = xc + cw_ref[CONV_WIDTH - 1 - k:CONV_WIDTH - k, :] * xl_scr[halo - k:halo - k + tc, :]
    xl_scr[0:halo, :] = xl_scr[tc:tc + halo, :]

    ra = _dot(xc.astype(BF16), wax_ref[...]) + bax_ref[...]
    r = _sigmoid(ra[:, 0:W])
    ig = _sigmoid(ra[:, W:2 * W])
    log_a = clam_ref[...] * r
    a = jnp.exp(log_a)
    mult = jnp.sqrt(-jnp.tanh(log_a) * (a * a + 1.0))
    b = mult * (ig * xc)
    nq = W // LANES
    for q in range(nq):
        a_scr[q] = a[:, q * LANES:(q + 1) * LANES]
        b_scr[q] = b[:, q * LANES:(q + 1) * LANES]

    h = [jnp.zeros((SUBLANES, LANES), F32)] * nq
    ap = [jnp.ones((SUBLANES, LANES), F32)] * nq
    for t in range(S):
        rows = pl.ds(t, SUBLANES, stride=S)
        for q in range(nq):
            at = a_scr[q, rows, :]
            h[q] = at * h[q] + b_scr[q, rows, :]
            ap[q] = at * ap[q]
            b_scr[q, rows, :] = h[q]
            a_scr[q, rows, :] = ap[q]
    for q in range(nq):
        l_scr[0:SUBLANES, q * LANES:(q + 1) * LANES] = h[q]
        l_scr[SUBLANES:2 * SUBLANES, q * LANES:(q + 1) * LANES] = ap[q]
    e = carry_scr[...]
    for c in range(SUBLANES):
        e_scr[c:c + 1, :] = e
        e = l_scr[c:c + 1, :] + l_scr[SUBLANES + c:SUBLANES + c + 1, :] * e
    carry_scr[...] = e
    for c in range(SUBLANES):
        rs = slice(c * S, (c + 1) * S)
        hl = jnp.concatenate([b_scr[q, rs, :] for q in range(nq)], axis=1)
        al = jnp.concatenate([a_scr[q, rs, :] for q in range(nq)], axis=1)
        hs = hl + al * e_scr[c:c + 1, :]
        o_ref[0, rs, :] = (_gelu(g_scr[rs, :]) * hs).astype(BF16)


def _lru_call(xb, win, cw, cb, wax, bax, clam):
    bsz, L, _ = xb.shape
    tc = ROW_TILE
    W = BRANCH_WIDTH
    return pl.pallas_call(
        _lru_kernel,
        grid=(bsz, L // tc),
        in_specs=[_row_spec(D_MODEL), _const_spec((D_MODEL, 2 * W)), _const_spec((CONV_WIDTH, W)),
                  _const_spec((1, W)), _const_spec((W, 2 * W)), _const_spec((1, 2 * W)),
                  _const_spec((1, W))],
        out_specs=_row_spec(W),
        out_shape=jax.ShapeDtypeStruct((bsz, L, W), BF16),
        scratch_shapes=[pltpu.VMEM((tc + 2 * SUBLANES, W), F32),
                        pltpu.VMEM((W // LANES, tc, LANES), F32),
                        pltpu.VMEM((W // LANES, tc, LANES), F32), pltpu.VMEM((tc, W), F32),
                        pltpu.VMEM((2 * SUBLANES, W), F32), pltpu.VMEM((SUBLANES, W), F32),
                        pltpu.VMEM((1, W), F32)],
        compiler_params=_params(2),
        name="lru",
    )(xb, win, cw, cb, wax, bax, clam)


def _fox_kernel(qa_ref, ka_ref, vt_ref, o_ref, acc_scr, ot_scr, m_scr, st_scr, mx_scr):
    T = qa_ref.shape[1]
    i = pl.program_id(1)
    dh = FOX_HEAD_DIM
    nt = (((1,), (1,)), ((), ()))
    key = lax.broadcasted_iota(jnp.int32, (T, T), 0)
    qry = lax.broadcasted_iota(jnp.int32, (T, T), 1)
    causal = key <= qry
    ones = jnp.ones((FOX_ACC_ROWS - dh, T), BF16)

    m_scr[...] = jnp.full(m_scr.shape, -jnp.inf, F32)
    acc_scr[...] = jnp.zeros(acc_scr.shape, F32)

    def scores(j, slot, masked):
        r0 = pl.multiple_of(j * T, T)
        for h in range(FOX_HEADS):
            hl = slice(h * LANES, (h + 1) * LANES)
            st = lax.dot_general(ka_ref[0, pl.ds(r0, T), hl], qa_ref[0, :, hl], nt,
                                 preferred_element_type=F32)
            if masked:
                st = jnp.where(causal, st, -jnp.inf)
            st_scr[slot, h] = st
            mx_scr[slot, h:h + 1, :] = jnp.max(st, axis=0, keepdims=True)

    def consume(j, slot):
        r0 = pl.multiple_of(j * T, T)
        m_all = m_scr[...]
        mx_all = mx_scr[slot]
        m_rows = []
        for h in range(FOX_HEADS):
            ha = slice(h * FOX_ACC_ROWS, (h + 1) * FOX_ACC_ROWS)
            m = m_all[h:h + 1, :]
            m_new = jnp.maximum(m, mx_all[h:h + 1, :])
            alpha = jnp.exp2(m - m_new)
            pt = jnp.exp2(st_scr[slot, h] - m_new).astype(BF16)
            m_rows.append(m_new)
            v1 = jnp.concatenate([vt_ref[0, h * dh:(h + 1) * dh, pl.ds(r0, T)], ones], axis=0)
            acc_scr[ha, :] = alpha * acc_scr[ha, :] + _dot(v1, pt)
        m_scr[...] = jnp.concatenate(m_rows, axis=0)

    npairs = jnp.maximum(i - 1, 0) // 2
    tail = 2 * npairs

    @pl.when(i > 0)
    def _():
        scores(0, 0, False)

    def body(p, carry):
        j = 2 * p
        scores(j + 1, 1, False)
        consume(j, 0)
        scores(j + 2, 0, False)
        consume(j + 1, 1)
        return carry

    lax.fori_loop(0, npairs, body, 0)

    @pl.when(i == 0)
    def _():
        scores(0, 0, True)
        consume(0, 0)

    @pl.when(i % 2 == 1)
    def _():
        scores(i, 1, True)
        consume(tail, 0)
        consume(i, 1)

    @pl.when(jnp.logical_and(i > 0, i % 2 == 0))
    def _():
        scores(tail + 1, 1, False)
        consume(tail, 0)
        scores(i, 0, True)
        consume(tail + 1, 1)
        consume(i, 0)
    for h in range(FOX_HEADS):
        a0 = h * FOX_ACC_ROWS
        ot_scr[h * dh:(h + 1) * dh, :] = acc_scr[a0:a0 + dh, :] / acc_scr[a0 + dh:a0 + dh + 1, :]
    o_ref[0] = ot_scr[...].T.astype(BF16)


def _fox_call(qa, ka, vt):
    bsz, L, n = qa.shape
    W = BRANCH_WIDTH
    T = ATT_TILE
    return pl.pallas_call(
        _fox_kernel,
        grid=(bsz, L // T),
        in_specs=[_row_spec(n, T),
                  pl.BlockSpec((1, L, n), lambda b, i: (b, 0, 0)),
                  pl.BlockSpec((1, W, L), lambda b, i: (b, 0, 0))],
        out_specs=_row_spec(W, T),
        out_shape=jax.ShapeDtypeStruct((bsz, L, W), BF16),
        scratch_shapes=[pltpu.VMEM((FOX_HEADS * FOX_ACC_ROWS, T), F32), pltpu.VMEM((W, T), F32),
                        pltpu.VMEM((FOX_HEADS, T), F32), pltpu.VMEM((2, FOX_HEADS, T, T), F32),
                        pltpu.VMEM((2, FOX_HEADS, T), F32)],
        compiler_params=_params(2),
        name="fox",
    )(qa, ka, vt)


def _merge_kernel(x_ref, xb_ref, y1_ref, y2_ref, y3_ref, wg_ref, bg_ref, wbr_ref, wo_ref,
                  g_ref, b_ref, o_ref, ob_ref):
    xb = xb_ref[0]
    mixed = None
    for k, y_ref in enumerate((y1_ref, y2_ref, y3_ref)):
        cs = slice(k * D_MODEL, (k + 1) * D_MODEL)
        gate = _sigmoid(_dot(xb, wg_ref[:, cs]) + bg_ref[:, cs])
        term = gate * _dot(y_ref[0], wbr_ref[k])
        mixed = term if mixed is None else mixed + term
    r = ALPHA * x_ref[0] + _dot(mixed.astype(BF16), wo_ref[...])
    y = _layer_norm(r, g_ref[...], b_ref[...])
    o_ref[0] = y
    ob_ref[0] = y.astype(BF16)


def _merge_call(x, xb, y1, y2, y3, wg, bg, wbr, wo, g, b):
    bsz, L, D = x.shape
    W = BRANCH_WIDTH
    return pl.pallas_call(
        _merge_kernel,
        grid=(bsz, L // ROW_TILE),
        in_specs=[_row_spec(D), _row_spec(D), _row_spec(W), _row_spec(W), _row_spec(W),
                  _const_spec((D, N_BRANCH * D)), _const_spec((1, N_BRANCH * D)),
                  _const_spec((N_BRANCH, W, D)), _const_spec((D, D)),
                  _const_spec((1, D)), _const_spec((1, D))],
        out_specs=[_row_spec(D), _row_spec(D)],
        out_shape=[jax.ShapeDtypeStruct((bsz, L, D), F32), jax.ShapeDtypeStruct((bsz, L, D), BF16)],
        compiler_params=_params(2),
        name="merge",
    )(x, xb, y1, y2, y3, wg, bg, wbr, wo, g, b)


def _ffn_kernel(x_ref, xb_ref, wg_ref, wu_ref, wd_ref, g_ref, b_ref, o_ref, ob_ref):
    xb = xb_ref[0]
    acc = None
    for c in range(FFN_HIDDEN // FFN_CHUNK):
        cs = slice(c * FFN_CHUNK, (c + 1) * FFN_CHUNK)
        gt = _dot(xb, wg_ref[:, cs])
        hid = (gt * _sigmoid(gt)) * _dot(xb, wu_ref[:, cs])
        term = _dot(hid.astype(BF16), wd_ref[cs, :])
        acc = term if acc is None else acc + term
    y = _layer_norm(ALPHA * x_ref[0] + acc, g_ref[...], b_ref[...])
    o_ref[0] = y
    ob_ref[0] = y.astype(BF16)


def _ffn_call(x, xb, wg, wu, wd, g, b):
    bsz, L, D = x.shape
    Hd = FFN_HIDDEN
    once = pl.Buffered(1)
    return pl.pallas_call(
        _ffn_kernel,
        grid=(bsz, L // ROW_TILE),
        in_specs=[_row_spec(D), _row_spec(D),
                  pl.BlockSpec((D, Hd), lambda b, i: (0, 0), pipeline_mode=once),
                  pl.BlockSpec((D, Hd), lambda b, i: (0, 0), pipeline_mode=once),
                  pl.BlockSpec((Hd, D), lambda b, i: (0, 0), pipeline_mode=once),
                  _const_spec((1, D)), _const_spec((1, D))],
        out_specs=[_row_spec(D), _row_spec(D)],
        out_shape=[jax.ShapeDtypeStruct((bsz, L, D), F32), jax.ShapeDtypeStruct((bsz, L, D), BF16)],
        compiler_params=_params(2),
        name="ffn",
    )(x, xb, wg, wu, wd, g, b)


def _s5_weights(a_re, a_im, log_dt, b_re, b_im, c_re, c_im, sub_len):
    lam = lax.complex(a_re, a_im)
    dt = jnp.exp(log_dt)[:, None]
    lam_bar = jnp.exp(lam * dt)
    b_bar = ((lam_bar - 1.0) / lam)[:, :, None] * lax.complex(b_re, b_im)
    c_c = lax.complex(c_re, c_im)
    eye = jnp.eye(S5_SLAB_GROUPS, dtype=F32)
    bb = b_bar.reshape(S5_SLABS, S5_SLAB_GROUPS, S5_STATE, S5_GROUP)
    cc = c_c.reshape(S5_SLABS, S5_SLAB_GROUPS, S5_GROUP, S5_STATE)

    def in_w(part):
        return jnp.einsum('bgpc,gh->bgchp', part, eye).reshape(S5_SLABS, S5_SLAB_IN, S5_HALF)

    def out_w(part):
        return jnp.einsum('bgcp,gh->bhpgc', part, eye).reshape(S5_SLABS, S5_HALF, S5_SLAB_IN)

    wb = jnp.concatenate([in_w(jnp.real(bb)), in_w(jnp.imag(bb))], axis=2).astype(BF16)
    wc = jnp.concatenate([out_w(jnp.real(cc)), -out_w(jnp.imag(cc))], axis=1).astype(BF16)
    pw = lax.associative_scan(jnp.multiply, jnp.broadcast_to(lam_bar, (sub_len,) + lam_bar.shape),
                              axis=0)
    pr = jnp.real(pw).reshape(sub_len, S5_NSTATE)
    pi = jnp.imag(pw).reshape(sub_len, S5_NSTATE)
    lr = jnp.broadcast_to(jnp.real(lam_bar).reshape(1, S5_NSTATE), (SUBLANES, S5_NSTATE))
    li = jnp.broadcast_to(jnp.imag(lam_bar).reshape(1, S5_NSTATE), (SUBLANES, S5_NSTATE))
    return wb, wc, lr, li, pr, pi


def _fox_selectors():
    n = FOX_HEADS * LANES
    selq = np.zeros((LANES, n), np.float32)
    selk = np.zeros((LANES, n), np.float32)
    oneq = np.zeros((1, n), np.float32)
    onek = np.zeros((1, n), np.float32)
    for h in range(FOX_HEADS):
        base = h * LANES + FOX_HEAD_DIM
        for part in range(3):
            selq[part * FOX_HEADS + h, base + part] = 1.0
            selk[part * FOX_HEADS + h, base + 3 + part] = -1.0
            oneq[0, base + 3 + part] = 1.0
            onek[0, base + part] = 1.0
    return (jnp.asarray(selq, BF16), jnp.asarray(selk, BF16), jnp.asarray(oneq), jnp.asarray(onek))


def _block_diag(w):
    n, d, _ = w.shape
    return jnp.einsum('hij,hg->higj', w, jnp.eye(n, dtype=w.dtype)).reshape(n * d, n * d)


def kernel(x, w_in, b_f, b_gate, s5_a_re, s5_a_im, s5_log_dt, s5_b_re, s5_b_im, s5_c_re, s5_c_im,
           s5_d, s5_w_glu, s5_b_glu, lru_conv_w, lru_conv_b, lru_w_a, lru_b_a, lru_w_x, lru_b_x,
           lru_lambda, w_branch, w_out, ln1_g, ln1_b, w_ffn_gate, w_ffn_up, w_ffn_down, ln2_g, ln2_b):
    bsz, L, D = x.shape
    W = BRANCH_WIDTH
    sub_len = ROW_TILE // SUBLANES
    tri = (lax.broadcasted_iota(jnp.int32, (ROW_TILE, ROW_TILE), 1)
           <= lax.broadcasted_iota(jnp.int32, (ROW_TILE, ROW_TILE), 0)).astype(BF16)
    o_s5, o_lx, o_q, o_f, o_g = 0, W, 3 * W, 6 * W, 6 * W + FOX_HEADS
    selq, selk, oneq, onek = _fox_selectors()
    pad_head = ((0, 0), (0, 0), (0, LANES - FOX_HEAD_DIM))
    xb = x.astype(BF16)
    for l in range(DEPTH):
        wl = w_in[l]
        w_s5 = wl[:, o_s5:o_lx].astype(BF16)
        w_lru = wl[:, o_lx:o_q].astype(BF16)
        wq = wl[:, o_q:o_q + W].reshape(D, FOX_HEADS, FOX_HEAD_DIM) * (LOG2E * FOX_HEAD_DIM ** -0.5)
        wk = wl[:, o_q + W:o_q + 2 * W].reshape(D, FOX_HEADS, FOX_HEAD_DIM)
        w_qk = jnp.concatenate([jnp.pad(wq, pad_head).reshape(D, FOX_HEADS * LANES),
                                jnp.pad(wk, pad_head).reshape(D, FOX_HEADS * LANES)],
                               axis=1).astype(BF16)
        w_vt = wl[:, o_q + 2 * W:o_f].T.astype(BF16)
        w_f = jnp.pad(wl[:, o_f:o_g], ((0, 0), (0, LANES - FOX_HEADS))).astype(BF16)
        w_gate = wl[:, o_g:].astype(BF16)
        bf = jnp.pad(b_f[l], (0, LANES - FOX_HEADS)).reshape(1, LANES)

        qa, ka, vt = _qkv_call(xb, w_qk, w_vt, w_f, bf, tri, selq, selk, oneq, onek)

        wb, wc, lr, li, pr, pi = _s5_weights(s5_a_re[l], s5_a_im[l], s5_log_dt[l], s5_b_re[l],
                                             s5_b_im[l], s5_c_re[l], s5_c_im[l], sub_len)
        y_s5 = _s5_call(xb, w_s5, wb, wc, lr, li, pr, pi, s5_d[l].reshape(1, W),
                        s5_w_glu[l].astype(BF16), s5_b_glu[l].reshape(1, W))

        wax = jnp.concatenate([_block_diag(lru_w_a[l]), _block_diag(lru_w_x[l])], axis=1).astype(BF16)
        bax = jnp.concatenate([lru_b_a[l].reshape(1, W), lru_b_x[l].reshape(1, W)], axis=1)
        clam = (-LRU_C * jax.nn.softplus(-lru_lambda[l])).reshape(1, W)
        y_lru = _lru_call(xb, w_lru, lru_conv_w[l], lru_conv_b[l].reshape(1, W), wax, bax, clam)

        y_fox = _fox_call(qa, ka, vt)

        x, xb = _merge_call(x, xb, y_s5, y_lru, y_fox, w_gate, b_gate[l].reshape(1, N_BRANCH * D),
                            w_branch[l].astype(BF16), w_out[l].astype(BF16),
                            ln1_g[l].reshape(1, D), ln1_b[l].reshape(1, D))
        x, xb = _ffn_call(x, xb, w_ffn_gate[l].astype(BF16), w_ffn_up[l].astype(BF16),
                          w_ffn_down[l].astype(BF16), ln2_g[l].reshape(1, D), ln2_b[l].reshape(1, D))
    return x
```

```python
import functools
import math

import jax
import jax.numpy as jnp
import numpy as np
from jax import lax
from jax.experimental import pallas as pl
from jax.experimental.pallas import tpu as pltpu

F32 = jnp.float32
BF16 = jnp.bfloat16

D_MODEL = 1024
DEPTH = 4
BRANCH_WIDTH = D_MODEL // 2
N_BRANCH = 3
S5_GROUP = 16
S5_GROUPS = BRANCH_WIDTH // S5_GROUP
S5_STATE = 64
LRU_HEADS = 8
LRU_HEAD_DIM = BRANCH_WIDTH // LRU_HEADS
LRU_C = 8.0
CONV_WIDTH = 4
FOX_HEAD_DIM = 64
FOX_HEADS = BRANCH_WIDTH // FOX_HEAD_DIM
FOX_ACC_ROWS = FOX_HEAD_DIM + 16
LOG2E = math.log2(math.e)
FFN_HIDDEN = ((8 * D_MODEL + 3 * 256 - 1) // (3 * 256)) * 256
ALPHA = (2.0 * DEPTH) ** 0.25
LN_EPS = 1e-5

LANES = 128
SUBLANES = 8
S5_SLABS = 4
S5_SLAB_GROUPS = S5_GROUPS // S5_SLABS
S5_HALF = S5_SLAB_GROUPS * S5_STATE
S5_SLAB_IN = S5_SLAB_GROUPS * S5_GROUP
S5_NSTATE = S5_GROUPS * S5_STATE

ROW_TILE = 512
ATT_TILE = 256
FFN_CHUNK = 1408
VMEM_LIMIT = 56 * 1024 * 1024


def _params(n_axes):
    return pltpu.CompilerParams(
        dimension_semantics=("arbitrary",) * n_axes, vmem_limit_bytes=VMEM_LIMIT)


def _dot(a, b):
    return jnp.dot(a, b, preferred_element_type=F32)


def _sigmoid(x):
    return 1.0 / (1.0 + jnp.exp(-x))


def _gelu(x):
    return 0.5 * x * (1.0 + jnp.tanh(math.sqrt(2.0 / math.pi) * (x + 0.044715 * (x * x * x))))


def _layer_norm(r, g, b):
    mu = jnp.mean(r, axis=-1, keepdims=True)
    d = r - mu
    var = jnp.mean(d * d, axis=-1, keepdims=True)
    return d * lax.rsqrt(var + LN_EPS) * g + b


def _const_spec(shape):
    return pl.BlockSpec(shape, lambda b, i: (0,) * len(shape))


def _row_spec(width, tile=ROW_TILE):
    return pl.BlockSpec((1, tile, width), lambda b, i: (b, i, 0))


def _split3(x):
    hi = x.astype(BF16)
    r1 = x - hi.astype(F32)
    mid = r1.astype(BF16)
    lo = (r1 - mid.astype(F32)).astype(BF16)
    return hi, mid, lo


def _qkv_kernel(xb_ref, wqk_ref, wvt_ref, wf_ref, bf_ref, tri_ref, selq_ref, selk_ref, oneq_ref,
                onek_ref, qa_ref, ka_ref, vt_ref, carry_ref):
    @pl.when(pl.program_id(1) == 0)
    def _():
        carry_ref[...] = jnp.zeros_like(carry_ref)

    tm = xb_ref.shape[1]
    x = xb_ref[0]
    f = _dot(x, wf_ref[...]) + bf_ref[...]
    lf = jnp.minimum(f, 0.0) - jnp.log1p(jnp.exp(-jnp.abs(f)))
    hi, mid, lo = _split3(lf)
    tri = tri_ref[...]
    cum = _dot(tri, hi) + _dot(tri, mid) + _dot(tri, lo) + carry_ref[...]
    carry_ref[...] = cum[tm - 1:tm, :]

    c_hi, c_mid, c_lo = _split3(cum * LOG2E)
    lane = lax.broadcasted_iota(jnp.int32, cum.shape, 1)
    h8 = FOX_HEADS
    c3 = jnp.where(lane < h8, c_hi.astype(F32),
                   jnp.where(lane < 2 * h8, pltpu.roll(c_mid.astype(F32), h8, 1),
                             jnp.where(lane < 3 * h8, pltpu.roll(c_lo.astype(F32), 2 * h8, 1), 0.0)))
    c3 = c3.astype(BF16)
    z = _dot(x, wqk_ref[...])
    n = FOX_HEADS * LANES
    qa_ref[0] = (z[:, 0:n] + _dot(c3, selq_ref[...]) + oneq_ref[...]).astype(BF16)
    ka_ref[0] = (z[:, n:2 * n] + _dot(c3, selk_ref[...]) + onek_ref[...]).astype(BF16)
    nt = (((1,), (1,)), ((), ()))
    vt_ref[0] = lax.dot_general(wvt_ref[...], x, nt, preferred_element_type=F32).astype(BF16)


def _qkv_call(xb, wqk, wvt, wf, bf, tri, selq, selk, oneq, onek):
    bsz, L, D = xb.shape
    n = FOX_HEADS * LANES
    W = BRANCH_WIDTH
    return pl.pallas_call(
        _qkv_kernel,
        grid=(bsz, L // ROW_TILE),
        in_specs=[_row_spec(D), _const_spec((D, 2 * n)), _const_spec((W, D)),
                  _const_spec((D, LANES)), _const_spec((1, LANES)),
                  _const_spec((ROW_TILE, ROW_TILE)), _const_spec((LANES, n)),
                  _const_spec((LANES, n)), _const_spec((1, n)), _const_spec((1, n))],
        out_specs=[_row_spec(n), _row_spec(n),
                   pl.BlockSpec((1, W, ROW_TILE), lambda b, i: (b, 0, i))],
        out_shape=[jax.ShapeDtypeStruct((bsz, L, n), BF16), jax.ShapeDtypeStruct((bsz, L, n), BF16),
                   jax.ShapeDtypeStruct((bsz, W, L), BF16)],
        scratch_shapes=[pltpu.VMEM((1, LANES), F32)],
        compiler_params=_params(2),
        name="qkv",
    )(xb, wqk, wvt, wf, bf, tri, selq, selk, oneq, onek)


def _cmul(ar, ai, br, bi):
    return ar * br - ai * bi, ar * bi + ai * br


def _s5_kernel(xb_ref, perm_ref, permt_ref, win_ref, wb_ref, wc_ref, lr_ref, li_ref, d_ref, wg_ref,
               bg_ref, o_ref, bu_scr, hb_scr, l_scr, e_scr, carry_scr):
    tc = xb_ref.shape[1]
    S = tc // SUBLANES
    H = S5_HALF
    nq = H // LANES

    @pl.when(pl.program_id(1) == 0)
    def _():
        carry_scr[...] = jnp.zeros_like(carry_scr)

    xp = _dot(perm_ref[...], xb_ref[0]).astype(BF16)
    u = _dot(xp, win_ref[...])
    ub = u.astype(BF16)
    for blk in range(S5_SLABS):
        bu_scr[:, blk * 2 * H:(blk + 1) * 2 * H] = _dot(
            ub[:, blk * S5_SLAB_IN:(blk + 1) * S5_SLAB_IN], wb_ref[blk])

    pair = 2
    for b0 in range(0, S5_SLABS, pair):
        cols = []
        for blk in range(b0, b0 + pair):
            for q in range(nq):
                c0 = blk * 2 * H + q * LANES
                l0 = blk * H + q * LANES
                cols.append((slice(c0, c0 + LANES), slice(c0 + H, c0 + H + LANES),
                             slice(l0, l0 + LANES)))
        n = len(cols)
        lr = [lr_ref[:, cl] for _, _, cl in cols]
        li = [li_ref[:, cl] for _, _, cl in cols]
        hr = [jnp.zeros((SUBLANES, LANES), F32)] * n
        hi = [jnp.zeros((SUBLANES, LANES), F32)] * n
        for t in range(S):
            rows = slice(t * SUBLANES, (t + 1) * SUBLANES)
            for k, (cr, ci, _) in enumerate(cols):
                pr, pi = _cmul(lr[k], li[k], hr[k], hi[k])
                hr[k] = pr + bu_scr[rows, cr]
                hi[k] = pi + bu_scr[rows, ci]
                bu_scr[rows, cr] = hr[k]
                bu_scr[rows, ci] = hi[k]
        for k, (cr, ci, _) in enumerate(cols):
            psr, psi = lr[k], li[k]
            for _ in range(S.bit_length() - 1):
                psr, psi = _cmul(psr, psi, psr, psi)
            psr, psi = psr[0:1, :], psi[0:1, :]
            l_scr[:, cr] = hr[k]
            l_scr[:, ci] = hi[k]
            er = carry_scr[:, cr]
            ei = carry_scr[:, ci]
            for c in range(SUBLANES):
                e_scr[c:c + 1, cr] = er
                e_scr[c:c + 1, ci] = ei
                pr, pi = _cmul(psr, psi, er, ei)
                er = l_scr[c:c + 1, cr] + pr
                ei = l_scr[c:c + 1, ci] + pi
            carry_scr[:, cr] = er
            carry_scr[:, ci] = ei
        gr = [e_scr[:, cr] for cr, _, _ in cols]
        gi = [e_scr[:, ci] for _, ci, _ in cols]
        for t2 in range(S // 2):
            rows2 = slice(2 * t2 * SUBLANES, (2 * t2 + 2) * SUBLANES)
            for k, (cr, ci, _) in enumerate(cols):
                outr, outi = [], []
                for t in (2 * t2, 2 * t2 + 1):
                    rows = slice(t * SUBLANES, (t + 1) * SUBLANES)
                    gr[k], gi[k] = _cmul(lr[k], li[k], gr[k], gi[k])
                    outr.append(bu_scr[rows, cr] + gr[k])
                    outi.append(bu_scr[rows, ci] + gi[k])
                hb_scr[rows2, cr] = jnp.concatenate(outr, axis=0).astype(BF16)
                hb_scr[rows2, ci] = jnp.concatenate(outi, axis=0).astype(BF16)

    ys = [_dot(hb_scr[:, blk * 2 * H:(blk + 1) * 2 * H], wc_ref[blk]) for blk in range(S5_SLABS)]
    y = jnp.concatenate(ys, axis=1) + d_ref[...] * u
    y = _gelu(y)
    y = y * _sigmoid(_dot(y.astype(BF16), wg_ref[...]) + bg_ref[...])
    o_ref[0] = _dot(permt_ref[...], y.astype(BF16)).astype(BF16)


def _s5_call(xb, perm, permt, win, wb, wc, lr, li, d, wg, bg):
    bsz, L, _ = xb.shape
    tc = ROW_TILE
    ncol = 2 * S5_NSTATE
    return pl.pallas_call(
        _s5_kernel,
        grid=(bsz, L // tc),
        in_specs=[_row_spec(D_MODEL), _const_spec((tc, tc)), _const_spec((tc, tc)),
                  _const_spec((D_MODEL, BRANCH_WIDTH)),
                  _const_spec((S5_SLABS, S5_SLAB_IN, 2 * S5_HALF)),
                  _const_spec((S5_SLABS, 2 * S5_HALF, S5_SLAB_IN)),
                  _const_spec((SUBLANES, S5_NSTATE)), _const_spec((SUBLANES, S5_NSTATE)),
                  _const_spec((1, BRANCH_WIDTH)), _const_spec((BRANCH_WIDTH, BRANCH_WIDTH)),
                  _const_spec((1, BRANCH_WIDTH))],
        out_specs=_row_spec(BRANCH_WIDTH),
        out_shape=jax.ShapeDtypeStruct((bsz, L, BRANCH_WIDTH), BF16),
        scratch_shapes=[pltpu.VMEM((tc, ncol), F32), pltpu.VMEM((tc, ncol), BF16),
                        pltpu.VMEM((SUBLANES, ncol), F32), pltpu.VMEM((SUBLANES, ncol), F32),
                        pltpu.VMEM((1, ncol), F32)],
        compiler_params=_params(2),
        name="s5",
    )(xb, perm, permt, win, wb, wc, lr, li, d, wg, bg)


def _lru_kernel(xb_ref, perm_ref, permt_ref, win_ref, cw_ref, cb_ref, wax_ref, bax_ref, clam_ref,
                o_ref, xl_scr, a_scr, b_scr, l_scr, e_scr, carry_scr, halo_scr):
    tc = xb_ref.shape[1]
    S = tc // SUBLANES
    W = BRANCH_WIDTH
    nback = CONV_WIDTH - 1
    pre = nback * SUBLANES

    @pl.when(pl.program_id(1) == 0)
    def _():
        halo_scr[...] = jnp.zeros_like(halo_scr)
        carry_scr[...] = jnp.zeros_like(carry_scr)

    xp = _dot(perm_ref[...], xb_ref[0]).astype(BF16)
    z = _dot(xp, win_ref[...])
    xl = z[:, 0:W]
    gate = z[:, W:2 * W]
    xl_scr[pre:pre + tc, :] = xl
    first = lax.broadcasted_iota(jnp.int32, (SUBLANES, W), 0) == 0
    for k in range(1, CONV_WIDTH):
        shifted = pltpu.roll(xl[(S - k) * SUBLANES:(S - k + 1) * SUBLANES, :], 1, 0)
        hrows = slice((k - 1) * SUBLANES, k * SUBLANES)
        xl_scr[(nback - k) * SUBLANES:(nback - k + 1) * SUBLANES, :] = jnp.where(
            first, halo_scr[hrows, :], shifted)
        halo_scr[hrows, :] = shifted
    xc = cb_ref[...] + cw_ref[nback:nback + 1, :] * xl
    for k in range(1, CONV_WIDTH):
        back = pre - k * SUBLANES
        xc = xc + cw_ref[nback - k:nback - k + 1, :] * xl_scr[back:back + tc, :]

    ra = _dot(xc.astype(BF16), wax_ref[...]) + bax_ref[...]
    r = _sigmoid(ra[:, 0:W])
    ig = _sigmoid(ra[:, W:2 * W])
    log_a = clam_ref[...] * r
    a = jnp.exp(log_a)
    mult = jnp.sqrt(-jnp.tanh(log_a) * (a * a + 1.0))
    a_scr[...] = a
    b_scr[...] = mult * (ig * xc)

    h = jnp.zeros((SUBLANES, W), F32)
    ap = jnp.ones((SUBLANES, W), F32)
    for t in range(S):
        rows = slice(t * SUBLANES, (t + 1) * SUBLANES)
        at = a_scr[rows, :]
        h = at * h + b_scr[rows, :]
        ap = at * ap
        b_scr[rows, :] = h
        a_scr[rows, :] = ap
    l_scr[0:SUBLANES, :] = h
    l_scr[SUBLANES:2 * SUBLANES, :] = ap
    e = carry_scr[...]
    for c in range(SUBLANES):
        e_scr[c:c + 1, :] = e
        e = l_scr[c:c + 1, :] + l_scr[SUBLANES + c:SUBLANES + c + 1, :] * e
    carry_scr[...] = e
    et = e_scr[...][None]
    hs = b_scr[...].reshape(S, SUBLANES, W) + a_scr[...].reshape(S, SUBLANES, W) * et
    y = (_gelu(gate) * hs.reshape(tc, W)).astype(BF16)
    o_ref[0] = _dot(permt_ref[...], y).astype(BF16)


def _lru_call(xb, perm, permt, win, cw, cb, wax, bax, clam):
    bsz, L, _ = xb.shape
    tc = ROW_TILE
    W = BRANCH_WIDTH
    nback = CONV_WIDTH - 1
    return pl.pallas_call(
        _lru_kernel,
        grid=(bsz, L // tc),
        in_specs=[_row_spec(D_MODEL), _const_spec((tc, tc)), _const_spec((tc, tc)),
                  _const_spec((D_MODEL, 2 * W)), _const_spec((CONV_WIDTH, W)),
                  _const_spec((1, W)), _const_spec((W, 2 * W)), _const_spec((1, 2 * W)),
                  _const_spec((1, W))],
        out_specs=_row_spec(W),
        out_shape=jax.ShapeDtypeStruct((bsz, L, W), BF16),
        scratch_shapes=[pltpu.VMEM((tc + nback * SUBLANES, W), F32), pltpu.VMEM((tc, W), F32),
                        pltpu.VMEM((tc, W), F32), pltpu.VMEM((2 * SUBLANES, W), F32),
                        pltpu.VMEM((SUBLANES, W), F32), pltpu.VMEM((1, W), F32),
                        pltpu.VMEM((nback * SUBLANES, W), F32)],
        compiler_params=_params(2),
        name="lru",
    )(xb, perm, permt, win, cw, cb, wax, bax, clam)


def _fox_kernel(qa_ref, ka_ref, vt_ref, o_ref, acc_scr, ot_scr, m_scr, st_scr, mx_scr):
    T = qa_ref.shape[1]
    i = pl.program_id(1)
    dh = FOX_HEAD_DIM
    nt = (((1,), (1,)), ((), ()))
    key = lax.broadcasted_iota(jnp.int32, (T, T), 0)
    qry = lax.broadcasted_iota(jnp.int32, (T, T), 1)
    causal = key <= qry
    ones = jnp.ones((FOX_ACC_ROWS - dh, T), BF16)

    m_scr[...] = jnp.full(m_scr.shape, -jnp.inf, F32)
    acc_scr[...] = jnp.zeros(acc_scr.shape, F32)

    def scores(j, slot, masked):
        r0 = pl.multiple_of(j * T, T)
        for h in range(FOX_HEADS):
            hl = slice(h * LANES, (h + 1) * LANES)
            st = lax.dot_general(ka_ref[0, pl.ds(r0, T), hl], qa_ref[0, :, hl], nt,
                                 preferred_element_type=F32)
            if masked:
                st = jnp.where(causal, st, -jnp.inf)
            st_scr[slot, h] = st
            mx_scr[slot, h:h + 1, :] = jnp.max(st, axis=0, keepdims=True)

    def consume(j, slot):
        r0 = pl.multiple_of(j * T, T)
        m_all = m_scr[...]
        mx_all = mx_scr[slot]
        m_rows = []
        for h in range(FOX_HEADS):
            ha = slice(h * FOX_ACC_ROWS, (h + 1) * FOX_ACC_ROWS)
            m = m_all[h:h + 1, :]
            m_new = jnp.maximum(m, mx_all[h:h + 1, :])
            alpha = jnp.exp2(m - m_new)
            pt = jnp.exp2(st_scr[slot, h] - m_new).astype(BF16)
            m_rows.append(m_new)
            v1 = jnp.concatenate([vt_ref[0, h * dh:(h + 1) * dh, pl.ds(r0, T)], ones], axis=0)
            acc_scr[ha, :] = alpha * acc_scr[ha, :] + _dot(v1, pt)
        m_scr[...] = jnp.concatenate(m_rows, axis=0)

    npairs = jnp.maximum(i - 1, 0) // 2
    tail = 2 * npairs

    @pl.when(i > 0)
    def _():
        scores(0, 0, False)

    def body(p, carry):
        j = 2 * p
        scores(j + 1, 1, False)
        consume(j, 0)
        scores(j + 2, 0, False)
        consume(j + 1, 1)
        return carry

    lax.fori_loop(0, npairs, body, 0)

    @pl.when(i == 0)
    def _():
        scores(0, 0, True)
        consume(0, 0)

    @pl.when(i % 2 == 1)
    def _():
        scores(i, 1, True)
        consume(tail, 0)
        consume(i, 1)

    @pl.when(jnp.logical_and(i > 0, i % 2 == 0))
    def _():
        scores(tail + 1, 1, False)
        consume(tail, 0)
        scores(i, 0, True)
        consume(tail + 1, 1)
        consume(i, 0)
    for h in range(FOX_HEADS):
        a0 = h * FOX_ACC_ROWS
        ot_scr[h * dh:(h + 1) * dh, :] = acc_scr[a0:a0 + dh, :] / acc_scr[a0 + dh:a0 + dh + 1, :]
    o_ref[0] = ot_scr[...].T.astype(BF16)


def _fox_call(qa, ka, vt):
    bsz, L, n = qa.shape
    W = BRANCH_WIDTH
    T = ATT_TILE
    return pl.pallas_call(
        _fox_kernel,
        grid=(bsz, L // T),
        in_specs=[_row_spec(n, T),
                  pl.BlockSpec((1, L, n), lambda b, i: (b, 0, 0)),
                  pl.BlockSpec((1, W, L), lambda b, i: (b, 0, 0))],
        out_specs=_row_spec(W, T),
        out_shape=jax.ShapeDtypeStruct((bsz, L, W), BF16),
        scratch_shapes=[pltpu.VMEM((FOX_HEADS * FOX_ACC_ROWS, T), F32), pltpu.VMEM((W, T), F32),
                        pltpu.VMEM((FOX_HEADS, T), F32), pltpu.VMEM((2, FOX_HEADS, T, T), F32),
                        pltpu.VMEM((2, FOX_HEADS, T), F32)],
        compiler_params=_params(2),
        name="fox",
    )(qa, ka, vt)


def _merge_kernel(x_ref, xb_ref, y1_ref, y2_ref, y3_ref, wg_ref, bg_ref, wbr_ref, wo_ref,
                  g_ref, b_ref, o_ref, ob_ref):
    xb = xb_ref[0]
    mixed = None
    for k, y_ref in enumerate((y1_ref, y2_ref, y3_ref)):
        cs = slice(k * D_MODEL, (k + 1) * D_MODEL)
        gate = _sigmoid(_dot(xb, wg_ref[:, cs]) + bg_ref[:, cs])
        term = gate * _dot(y_ref[0], wbr_ref[k])
        mixed = term if mixed is None else mixed + term
    r = ALPHA * x_ref[0] + _dot(mixed.astype(BF16), wo_ref[...])
    y = _layer_norm(r, g_ref[...], b_ref[...])
    o_ref[0] = y
    ob_ref[0] = y.astype(BF16)


def _merge_call(x, xb, y1, y2, y3, wg, bg, wbr, wo, g, b):
    bsz, L, D = x.shape
    W = BRANCH_WIDTH
    return pl.pallas_call(
        _merge_kernel,
        grid=(bsz, L // ROW_TILE),
        in_specs=[_row_spec(D), _row_spec(D), _row_spec(W), _row_spec(W), _row_spec(W),
                  _const_spec((D, N_BRANCH * D)), _const_spec((1, N_BRANCH * D)),
                  _const_spec((N_BRANCH, W, D)), _const_spec((D, D)),
                  _const_spec((1, D)), _const_spec((1, D))],
        out_specs=[_row_spec(D), _row_spec(D)],
        out_shape=[jax.ShapeDtypeStruct((bsz, L, D), F32), jax.ShapeDtypeStruct((bsz, L, D), BF16)],
        compiler_params=_params(2),
        name="merge",
    )(x, xb, y1, y2, y3, wg, bg, wbr, wo, g, b)


def _ffn_kernel(x_ref, xb_ref, wg_ref, wu_ref, wd_ref, g_ref, b_ref, o_ref, ob_ref):
    xb = xb_ref[0]
    acc = None
    for c in range(FFN_HIDDEN // FFN_CHUNK):
        cs = slice(c * FFN_CHUNK, (c + 1) * FFN_CHUNK)
        gt = _dot(xb, wg_ref[:, cs])
        hid = (gt * _sigmoid(gt)) * _dot(xb, wu_ref[:, cs])
        term = _dot(hid.astype(BF16), wd_ref[cs, :])
        acc = term if acc is None else acc + term
    y = _layer_norm(ALPHA * x_ref[0] + acc, g_ref[...], b_ref[...])
    o_ref[0] = y
    ob_ref[0] = y.astype(BF16)


def _ffn_call(x, xb, wg, wu, wd, g, b):
    bsz, L, D = x.shape
    Hd = FFN_HIDDEN
    once = pl.Buffered(1)
    return pl.pallas_call(
        _ffn_kernel,
        grid=(bsz, L // ROW_TILE),
        in_specs=[_row_spec(D), _row_spec(D),
                  pl.BlockSpec((D, Hd), lambda b, i: (0, 0), pipeline_mode=once),
                  pl.BlockSpec((D, Hd), lambda b, i: (0, 0), pipeline_mode=once),
                  pl.BlockSpec((Hd, D), lambda b, i: (0, 0), pipeline_mode=once),
                  _const_spec((1, D)), _const_spec((1, D))],
        out_specs=[_row_spec(D), _row_spec(D)],
        out_shape=[jax.ShapeDtypeStruct((bsz, L, D), F32), jax.ShapeDtypeStruct((bsz, L, D), BF16)],
        compiler_params=_params(2),
        name="ffn",
    )(x, xb, wg, wu, wd, g, b)


def _s5_weights(a_re, a_im, log_dt, b_re, b_im, c_re, c_im):
    dt = jnp.exp(log_dt)[:, None]
    mag = jnp.exp(a_re * dt)
    lbr = mag * jnp.cos(a_im * dt)
    lbi = mag * jnp.sin(a_im * dt)
    den = a_re * a_re + a_im * a_im
    kr = ((lbr - 1.0) * a_re + lbi * a_im) / den
    ki = (lbi * a_re - (lbr - 1.0) * a_im) / den
    bbr = kr[:, :, None] * b_re - ki[:, :, None] * b_im
    bbi = kr[:, :, None] * b_im + ki[:, :, None] * b_re
    eye = jnp.eye(S5_SLAB_GROUPS, dtype=F32)

    def in_w(part):
        part = part.reshape(S5_SLABS, S5_SLAB_GROUPS, S5_STATE, S5_GROUP)
        return jnp.einsum('bgpc,gh->bgchp', part, eye).reshape(S5_SLABS, S5_SLAB_IN, S5_HALF)

    def out_w(part):
        part = part.reshape(S5_SLABS, S5_SLAB_GROUPS, S5_GROUP, S5_STATE)
        return jnp.einsum('bgcp,gh->bhpgc', part, eye).reshape(S5_SLABS, S5_HALF, S5_SLAB_IN)

    wb = jnp.concatenate([in_w(bbr), in_w(bbi)], axis=2).astype(BF16)
    wc = jnp.concatenate([out_w(c_re), -out_w(c_im)], axis=1).astype(BF16)
    lr = jnp.broadcast_to(lbr.reshape(1, S5_NSTATE), (SUBLANES, S5_NSTATE))
    li = jnp.broadcast_to(lbi.reshape(1, S5_NSTATE), (SUBLANES, S5_NSTATE))
    return wb, wc, lr, li


def _interleave_perm(tc):
    S = tc // SUBLANES
    p = np.zeros((tc, tc), np.float32)
    for c in range(SUBLANES):
        for t in range(S):
            p[SUBLANES * t + c, c * S + t] = 1.0
    return jnp.asarray(p, BF16), jnp.asarray(p.T, BF16)


def _fox_selectors():
    n = FOX_HEADS * LANES
    selq = np.zeros((LANES, n), np.float32)
    selk = np.zeros((LANES, n), np.float32)
    oneq = np.zeros((1, n), np.float32)
    onek = np.zeros((1, n), np.float32)
    for h in range(FOX_HEADS):
        base = h * LANES + FOX_HEAD_DIM
        for part in range(3):
            selq[part * FOX_HEADS + h, base + part] = 1.0
            selk[part * FOX_HEADS + h, base + 3 + part] = -1.0
            oneq[0, base + 3 + part] = 1.0
            onek[0, base + part] = 1.0
    return (jnp.asarray(selq, BF16), jnp.asarray(selk, BF16), jnp.asarray(oneq), jnp.asarray(onek))


def _block_diag(w):
    n, d, _ = w.shape
    return jnp.einsum('hij,hg->higj', w, jnp.eye(n, dtype=w.dtype)).reshape(n * d, n * d)


def kernel(x, w_in, b_f, b_gate, s5_a_re, s5_a_im, s5_log_dt, s5_b_re, s5_b_im, s5_c_re, s5_c_im,
           s5_d, s5_w_glu, s5_b_glu, lru_conv_w, lru_conv_b, lru_w_a, lru_b_a, lru_w_x, lru_b_x,
           lru_lambda, w_branch, w_out, ln1_g, ln1_b, w_ffn_gate, w_ffn_up, w_ffn_down, ln2_g, ln2_b):
    bsz, L, D = x.shape
    W = BRANCH_WIDTH
    perm, permt = _interleave_perm(ROW_TILE)
    tri = (lax.broadcasted_iota(jnp.int32, (ROW_TILE, ROW_TILE), 1)
           <= lax.broadcasted_iota(jnp.int32, (ROW_TILE, ROW_TILE), 0)).astype(BF16)
    o_s5, o_lx, o_q, o_f, o_g = 0, W, 3 * W, 6 * W, 6 * W + FOX_HEADS
    selq, selk, oneq, onek = _fox_selectors()
    pad_head = ((0, 0), (0, 0), (0, LANES - FOX_HEAD_DIM))
    xb = x.astype(BF16)
    for l in range(DEPTH):
        wl = w_in[l]
        w_s5 = wl[:, o_s5:o_lx].astype(BF16)
        w_lru = wl[:, o_lx:o_q].astype(BF16)
        wq = wl[:, o_q:o_q + W].reshape(D, FOX_HEADS, FOX_HEAD_DIM) * (LOG2E * FOX_HEAD_DIM ** -0.5)
        wk = wl[:, o_q + W:o_q + 2 * W].reshape(D, FOX_HEADS, FOX_HEAD_DIM)
        w_qk = jnp.concatenate([jnp.pad(wq, pad_head).reshape(D, FOX_HEADS * LANES),
                                jnp.pad(wk, pad_head).reshape(D, FOX_HEADS * LANES)],
                               axis=1).astype(BF16)
        w_vt = wl[:, o_q + 2 * W:o_f].T.astype(BF16)
        w_f = jnp.pad(wl[:, o_f:o_g], ((0, 0), (0, LANES - FOX_HEADS))).astype(BF16)
        w_gate = wl[:, o_g:].astype(BF16)
        bf = jnp.pad(b_f[l], (0, LANES - FOX_HEADS)).reshape(1, LANES)

        qa, ka, vt = _qkv_call(xb, w_qk, w_vt, w_f, bf, tri, selq, selk, oneq, onek)

        wb, wc, lr, li = _s5_weights(s5_a_re[l], s5_a_im[l], s5_log_dt[l], s5_b_re[l],
                                     s5_b_im[l], s5_c_re[l], s5_c_im[l])
        y_s5 = _s5_call(xb, perm, permt, w_s5, wb, wc, lr, li, s5_d[l].reshape(1, W),
                        s5_w_glu[l].astype(BF16), s5_b_glu[l].reshape(1, W))

        wax = jnp.concatenate([_block_diag(lru_w_a[l]), _block_diag(lru_w_x[l])], axis=1).astype(BF16)
        bax = jnp.concatenate([lru_b_a[l].reshape(1, W), lru_b_x[l].reshape(1, W)], axis=1)
        clam = (-LRU_C * jax.nn.softplus(-lru_lambda[l])).reshape(1, W)
        y_lru = _lru_call(xb, perm, permt, w_lru, lru_conv_w[l], lru_conv_b[l].reshape(1, W), wax, bax, clam)

        y_fox = _fox_call(qa, ka, vt)

        x, xb = _merge_call(x, xb, y_s5, y_lru, y_fox, w_gate, b_gate[l].reshape(1, N_BRANCH * D),
                            w_branch[l].astype(BF16), w_out[l].astype(BF16),
                            ln1_g[l].reshape(1, D), ln1_b[l].reshape(1, D))
        x, xb = _ffn_call(x, xb, w_ffn_gate[l].astype(BF16), w_ffn_up[l].astype(BF16),
                          w_ffn_down[l].astype(BF16), ln2_g[l].reshape(1, D), ln2_b[l].reshape(1, D))
    return x
```

```python
import functools
import math

import jax
import jax.numpy as jnp
import numpy as np
from jax import lax
from jax.experimental import pallas as pl
from jax.experimental.pallas import tpu as pltpu

F32 = jnp.float32
BF16 = jnp.bfloat16

D_MODEL = 1024
DEPTH = 4
BRANCH_WIDTH = D_MODEL // 2
N_BRANCH = 3
S5_GROUP = 16
S5_GROUPS = BRANCH_WIDTH // S5_GROUP
S5_STATE = 64
LRU_HEADS = 8
LRU_HEAD_DIM = BRANCH_WIDTH // LRU_HEADS
LRU_C = 8.0
CONV_WIDTH = 4
FOX_HEAD_DIM = 64
FOX_HEADS = BRANCH_WIDTH // FOX_HEAD_DIM
FOX_ACC_ROWS = FOX_HEAD_DIM + 16
LOG2E = math.log2(math.e)
FFN_HIDDEN = ((8 * D_MODEL + 3 * 256 - 1) // (3 * 256)) * 256
ALPHA = (2.0 * DEPTH) ** 0.25
LN_EPS = 1e-5

LANES = 128
SUBLANES = 8
S5_SLABS = 4
S5_SLAB_GROUPS = S5_GROUPS // S5_SLABS
S5_HALF = S5_SLAB_GROUPS * S5_STATE
S5_SLAB_IN = S5_SLAB_GROUPS * S5_GROUP
S5_NSTATE = S5_GROUPS * S5_STATE

ROW_TILE = 512
ATT_TILE = 256
FFN_CHUNK = 1408
VMEM_LIMIT = 56 * 1024 * 1024


def _params(n_axes):
    return pltpu.CompilerParams(
        dimension_semantics=("arbitrary",) * n_axes, vmem_limit_bytes=VMEM_LIMIT)


def _dot(a, b):
    return jnp.dot(a, b, preferred_element_type=F32)


def _sigmoid(x):
    return 1.0 / (1.0 + jnp.exp(-x))


def _gelu(x):
    return 0.5 * x * (1.0 + jnp.tanh(math.sqrt(2.0 / math.pi) * (x + 0.044715 * (x * x * x))))


def _layer_norm(r, g, b):
    mu = jnp.mean(r, axis=-1, keepdims=True)
    d = r - mu
    var = jnp.mean(d * d, axis=-1, keepdims=True)
    return d * lax.rsqrt(var + LN_EPS) * g + b


def _const_spec(shape):
    return pl.BlockSpec(shape, lambda b, i: (0,) * len(shape))


def _row_spec(width, tile=ROW_TILE):
    return pl.BlockSpec((1, tile, width), lambda b, i: (b, i, 0))


def _split3(x):
    hi = x.astype(BF16)
    r1 = x - hi.astype(F32)
    mid = r1.astype(BF16)
    lo = (r1 - mid.astype(F32)).astype(BF16)
    return hi, mid, lo


def _cmul(ar, ai, br, bi):
    return ar * br - ai * bi, ar * bi + ai * br


def _s5_scan(bu_scr, hb_scr, l_scr, e_scr, carry_scr, lr_ref, li_ref, S):
    H = S5_HALF
    nq = H // LANES
    pair = 2
    for b0 in range(0, S5_SLABS, pair):
        cols = []
        for blk in range(b0, b0 + pair):
            for q in range(nq):
                c0 = blk * 2 * H + q * LANES
                l0 = blk * H + q * LANES
                cols.append((slice(c0, c0 + LANES), slice(c0 + H, c0 + H + LANES),
                             slice(l0, l0 + LANES)))
        n = len(cols)
        lr = [lr_ref[:, cl] for _, _, cl in cols]
        li = [li_ref[:, cl] for _, _, cl in cols]
        hr = [jnp.zeros((SUBLANES, LANES), F32)] * n
        hi = [jnp.zeros((SUBLANES, LANES), F32)] * n
        for t in range(S):
            rows = slice(t * SUBLANES, (t + 1) * SUBLANES)
            for k, (cr, ci, _) in enumerate(cols):
                pr, pi = _cmul(lr[k], li[k], hr[k], hi[k])
                hr[k] = pr + bu_scr[rows, cr]
                hi[k] = pi + bu_scr[rows, ci]
                bu_scr[rows, cr] = hr[k]
                bu_scr[rows, ci] = hi[k]
        for k, (cr, ci, _) in enumerate(cols):
            psr, psi = lr[k], li[k]
            for _ in range(S.bit_length() - 1):
                psr, psi = _cmul(psr, psi, psr, psi)
            psr, psi = psr[0:1, :], psi[0:1, :]
            l_scr[:, cr] = hr[k]
            l_scr[:, ci] = hi[k]
            er = carry_scr[:, cr]
            ei = carry_scr[:, ci]
            for c in range(SUBLANES):
                e_scr[c:c + 1, cr] = er
                e_scr[c:c + 1, ci] = ei
                pr, pi = _cmul(psr, psi, er, ei)
                er = l_scr[c:c + 1, cr] + pr
                ei = l_scr[c:c + 1, ci] + pi
            carry_scr[:, cr] = er
            carry_scr[:, ci] = ei
        gr = [e_scr[:, cr] for cr, _, _ in cols]
        gi = [e_scr[:, ci] for _, ci, _ in cols]
        for t2 in range(S // 2):
            rows2 = slice(2 * t2 * SUBLANES, (2 * t2 + 2) * SUBLANES)
            for k, (cr, ci, _) in enumerate(cols):
                outr, outi = [], []
                for t in (2 * t2, 2 * t2 + 1):
                    rows = slice(t * SUBLANES, (t + 1) * SUBLANES)
                    gr[k], gi[k] = _cmul(lr[k], li[k], gr[k], gi[k])
                    outr.append(bu_scr[rows, cr] + gr[k])
                    outi.append(bu_scr[rows, ci] + gi[k])
                hb_scr[rows2, cr] = jnp.concatenate(outr, axis=0).astype(BF16)
                hb_scr[rows2, ci] = jnp.concatenate(outi, axis=0).astype(BF16)


def _branch_kernel(xb_ref, perm_ref, permt_ref,
                   s_win_ref, s_wb_ref, s_wc_ref, s_lr_ref, s_li_ref, s_d_ref, s_wg_ref, s_bg_ref,
                   r_win_ref, r_cw_ref, r_cb_ref, r_wax_ref, r_bax_ref, r_clam_ref,
                   wqk_ref, wvt_ref, wf_ref, bf_ref, tri_ref, selq_ref, selk_ref, oneq_ref, onek_ref,
                   ys_ref, yr_ref, qa_ref, ka_ref, vt_ref,
                   bu_scr, hb_scr, sl_scr, se_scr, scarry_scr, u_scr,
                   xl_scr, a_scr, b_scr, g_scr, rl_scr, re_scr, rcarry_scr, halo_scr, fcarry_scr):
    tc = xb_ref.shape[1]
    S = tc // SUBLANES
    W = BRANCH_WIDTH
    H = S5_HALF
    nback = CONV_WIDTH - 1
    pre = nback * SUBLANES

    @pl.when(pl.program_id(1) == 0)
    def _():
        scarry_scr[...] = jnp.zeros_like(scarry_scr)
        rcarry_scr[...] = jnp.zeros_like(rcarry_scr)
        halo_scr[...] = jnp.zeros_like(halo_scr)
        fcarry_scr[...] = jnp.zeros_like(fcarry_scr)

    x = xb_ref[0]
    xp = _dot(perm_ref[...], x).astype(BF16)

    u = _dot(xp, s_win_ref[...])
    u_scr[...] = u
    ub = u.astype(BF16)
    for blk in range(S5_SLABS):
        bu_scr[:, blk * 2 * H:(blk + 1) * 2 * H] = _dot(
            ub[:, blk * S5_SLAB_IN:(blk + 1) * S5_SLAB_IN], s_wb_ref[blk])

    z = _dot(xp, r_win_ref[...])
    xl = z[:, 0:W]
    g_scr[...] = z[:, W:2 * W]
    xl_scr[pre:pre + tc, :] = xl
    first = lax.broadcasted_iota(jnp.int32, (SUBLANES, W), 0) == 0
    for k in range(1, CONV_WIDTH):
        shifted = pltpu.roll(xl[(S - k) * SUBLANES:(S - k + 1) * SUBLANES, :], 1, 0)
        hrows = slice((k - 1) * SUBLANES, k * SUBLANES)
        xl_scr[(nback - k) * SUBLANES:(nback - k + 1) * SUBLANES, :] = jnp.where(
            first, halo_scr[hrows, :], shifted)
        halo_scr[hrows, :] = shifted
    xc = r_cb_ref[...] + r_cw_ref[nback:nback + 1, :] * xl
    for k in range(1, CONV_WIDTH):
        back = pre - k * SUBLANES
        xc = xc + r_cw_ref[nback - k:nback - k + 1, :] * xl_scr[back:back + tc, :]
    ra = _dot(xc.astype(BF16), r_wax_ref[...]) + r_bax_ref[...]
    r = _sigmoid(ra[:, 0:W])
    ig = _sigmoid(ra[:, W:2 * W])
    log_a = r_clam_ref[...] * r
    a = jnp.exp(log_a)
    mult = jnp.sqrt(-jnp.tanh(log_a) * (a * a + 1.0))
    a_scr[...] = a
    b_scr[...] = mult * (ig * xc)

    f = _dot(x, wf_ref[...]) + bf_ref[...]
    lf = jnp.minimum(f, 0.0) - jnp.log1p(jnp.exp(-jnp.abs(f)))
    hi, mid, lo = _split3(lf)
    tri = tri_ref[...]
    cum = _dot(tri, hi) + _dot(tri, mid) + _dot(tri, lo) + fcarry_scr[...]
    fcarry_scr[...] = cum[tc - 1:tc, :]
    c_hi, c_mid, c_lo = _split3(cum * LOG2E)
    lane = lax.broadcasted_iota(jnp.int32, cum.shape, 1)
    h8 = FOX_HEADS
    c3 = jnp.where(lane < h8, c_hi.astype(F32),
                   jnp.where(lane < 2 * h8, pltpu.roll(c_mid.astype(F32), h8, 1),
                             jnp.where(lane < 3 * h8, pltpu.roll(c_lo.astype(F32), 2 * h8, 1), 0.0)))
    c3 = c3.astype(BF16)
    low = lax.broadcasted_iota(jnp.int32, (tc, LANES), 1) < FOX_HEAD_DIM

    def augment(zz, sel_ref, one_ref, out_ref):
        extra = _dot(c3, sel_ref[...]) + one_ref[...]
        for hp in range(FOX_HEADS // 2):
            two = zz[:, hp * LANES:(hp + 1) * LANES]
            for odd in range(2):
                h = 2 * hp + odd
                src = pltpu.roll(two, FOX_HEAD_DIM, 1) if odd else two
                out_ref[0, :, h * LANES:(h + 1) * LANES] = jnp.where(
                    low, src, extra[:, h * LANES:(h + 1) * LANES]).astype(BF16)

    augment(_dot(x, wqk_ref[:, 0:W]), selq_ref, oneq_ref, qa_ref)
    augment(_dot(x, wqk_ref[:, W:2 * W]), selk_ref, onek_ref, ka_ref)
    nt = (((1,), (1,)), ((), ()))
    vt_ref[0] = lax.dot_general(wvt_ref[...], x, nt, preferred_element_type=F32).astype(BF16)

    _s5_scan(bu_scr, hb_scr, sl_scr, se_scr, scarry_scr, s_lr_ref, s_li_ref, S)
    h = jnp.zeros((SUBLANES, W), F32)
    ap = jnp.ones((SUBLANES, W), F32)
    for t in range(S):
        rows = slice(t * SUBLANES, (t + 1) * SUBLANES)
        at = a_scr[rows, :]
        h = at * h + b_scr[rows, :]
        ap = at * ap
        b_scr[rows, :] = h
        a_scr[rows, :] = ap
    rl_scr[0:SUBLANES, :] = h
    rl_scr[SUBLANES:2 * SUBLANES, :] = ap
    e = rcarry_scr[...]
    for c in range(SUBLANES):
        re_scr[c:c + 1, :] = e
        e = rl_scr[c:c + 1, :] + rl_scr[SUBLANES + c:SUBLANES + c + 1, :] * e
    rcarry_scr[...] = e

    ys = [_dot(hb_scr[:, blk * 2 * H:(blk + 1) * 2 * H], s_wc_ref[blk]) for blk in range(S5_SLABS)]
    y = jnp.concatenate(ys, axis=1) + s_d_ref[...] * u_scr[...]
    y = _gelu(y)
    y = y * _sigmoid(_dot(y.astype(BF16), s_wg_ref[...]) + s_bg_ref[...])
    ys_ref[0] = _dot(permt_ref[...], y.astype(BF16)).astype(BF16)
    et = re_scr[...][None]
    hs = b_scr[...].reshape(S, SUBLANES, W) + a_scr[...].reshape(S, SUBLANES, W) * et
    yr = (_gelu(g_scr[...]) * hs.reshape(tc, W)).astype(BF16)
    yr_ref[0] = _dot(permt_ref[...], yr).astype(BF16)


def _branch_call(xb, perm, permt, s5_args, lru_args, qkv_args):
    bsz, L, D = xb.shape
    tc = ROW_TILE
    W = BRANCH_WIDTH
    n = FOX_HEADS * LANES
    ncol = 2 * S5_NSTATE
    nback = CONV_WIDTH - 1
    args = (xb, perm, permt) + tuple(s5_args) + tuple(lru_args) + tuple(qkv_args)
    once = pl.Buffered(1)
    in_specs = [_row_spec(D)] + [
        pl.BlockSpec(a.shape, functools.partial(lambda nd, b, i: (0,) * nd, a.ndim), pipeline_mode=once)
        for a in args[1:]]
    f32 = lambda *s: pltpu.VMEM(s, F32)
    return pl.pallas_call(
        _branch_kernel,
        grid=(bsz, L // tc),
        in_specs=in_specs,
        out_specs=[_row_spec(W), _row_spec(W), _row_spec(n), _row_spec(n),
                   pl.BlockSpec((1, W, tc), lambda b, i: (b, 0, i))],
        out_shape=[jax.ShapeDtypeStruct((bsz, L, W), BF16), jax.ShapeDtypeStruct((bsz, L, W), BF16),
                   jax.ShapeDtypeStruct((bsz, L, n), BF16), jax.ShapeDtypeStruct((bsz, L, n), BF16),
                   jax.ShapeDtypeStruct((bsz, W, L), BF16)],
        scratch_shapes=[f32(tc, ncol), pltpu.VMEM((tc, ncol), BF16), f32(SUBLANES, ncol),
                        f32(SUBLANES, ncol), f32(1, ncol), f32(tc, W),
                        f32(tc + nback * SUBLANES, W), f32(tc, W), f32(tc, W), f32(tc, W),
                        f32(2 * SUBLANES, W), f32(SUBLANES, W), f32(1, W), f32(nback * SUBLANES, W),
                        f32(1, LANES)],
        compiler_params=_params(2),
        name="branch",
    )(*args)


def _fox_kernel(qa_ref, ka_ref, vt_ref, o_ref, acc_scr, ot_scr, m_scr, st_scr, mx_scr):
    T = qa_ref.shape[1]
    i = pl.program_id(1)
    dh = FOX_HEAD_DIM
    nt = (((1,), (1,)), ((), ()))
    key = lax.broadcasted_iota(jnp.int32, (T, T), 0)
    qry = lax.broadcasted_iota(jnp.int32, (T, T), 1)
    causal = key <= qry
    ones = jnp.ones((FOX_ACC_ROWS - dh, T), BF16)

    m_scr[...] = jnp.full(m_scr.shape, -jnp.inf, F32)
    acc_scr[...] = jnp.zeros(acc_scr.shape, F32)

    def scores(j, slot, masked):
        r0 = pl.multiple_of(j * T, T)
        for h in range(FOX_HEADS):
            hl = slice(h * LANES, (h + 1) * LANES)
            st = lax.dot_general(ka_ref[0, pl.ds(r0, T), hl], qa_ref[0, :, hl], nt,
                                 preferred_element_type=F32)
            if masked:
                st = jnp.where(causal, st, -jnp.inf)
            st_scr[slot, h] = st
            mx_scr[slot, h:h + 1, :] = jnp.max(st, axis=0, keepdims=True)

    def consume(j, slot):
        r0 = pl.multiple_of(j * T, T)
        m_all = m_scr[...]
        mx_all = mx_scr[slot]
        m_rows = []
        for h in range(FOX_HEADS):
            ha = slice(h * FOX_ACC_ROWS, (h + 1) * FOX_ACC_ROWS)
            m = m_all[h:h + 1, :]
            m_new = jnp.maximum(m, mx_all[h:h + 1, :])
            alpha = jnp.exp2(m - m_new)
            pt = jnp.exp2(st_scr[slot, h] - m_new).astype(BF16)
            m_rows.append(m_new)
            v1 = jnp.concatenate([vt_ref[0, h * dh:(h + 1) * dh, pl.ds(r0, T)], ones], axis=0)
            acc_scr[ha, :] = alpha * acc_scr[ha, :] + _dot(v1, pt)
        m_scr[...] = jnp.concatenate(m_rows, axis=0)

    npairs = jnp.maximum(i - 1, 0) // 2
    tail = 2 * npairs

    @pl.when(i > 0)
    def _():
        scores(0, 0, False)

    def body(p, carry):
        j = 2 * p
        scores(j + 1, 1, False)
        consume(j, 0)
        scores(j + 2, 0, False)
        consume(j + 1, 1)
        return carry

    lax.fori_loop(0, npairs, body, 0)

    @pl.when(i == 0)
    def _():
        scores(0, 0, True)
        consume(0, 0)

    @pl.when(i % 2 == 1)
    def _():
        scores(i, 1, True)
        consume(tail, 0)
        consume(i, 1)

    @pl.when(jnp.logical_and(i > 0, i % 2 == 0))
    def _():
        scores(tail + 1, 1, False)
        consume(tail, 0)
        scores(i, 0, True)
        consume(tail + 1, 1)
        consume(i, 0)
    for h in range(FOX_HEADS):
        a0 = h * FOX_ACC_ROWS
        ot_scr[h * dh:(h + 1) * dh, :] = acc_scr[a0:a0 + dh, :] / acc_scr[a0 + dh:a0 + dh + 1, :]
    o_ref[0] = ot_scr[...].T.astype(BF16)


def _fox_call(qa, ka, vt):
    bsz, L, n = qa.shape
    W = BRANCH_WIDTH
    T = ATT_TILE
    return pl.pallas_call(
        _fox_kernel,
        grid=(bsz, L // T),
        in_specs=[_row_spec(n, T),
                  pl.BlockSpec((1, L, n), lambda b, i: (b, 0, 0)),
                  pl.BlockSpec((1, W, L), lambda b, i: (b, 0, 0))],
        out_specs=_row_spec(W, T),
        out_shape=jax.ShapeDtypeStruct((bsz, L, W), BF16),
        scratch_shapes=[pltpu.VMEM((FOX_HEADS * FOX_ACC_ROWS, T), F32), pltpu.VMEM((W, T), F32),
                        pltpu.VMEM((FOX_HEADS, T), F32), pltpu.VMEM((2, FOX_HEADS, T, T), F32),
                        pltpu.VMEM((2, FOX_HEADS, T), F32)],
        compiler_params=_params(2),
        name="fox",
    )(qa, ka, vt)


def _merge_kernel(x_ref, xb_ref, y1_ref, y2_ref, y3_ref, wg_ref, bg_ref, wbr_ref, wo_ref,
                  g_ref, b_ref, o_ref, ob_ref):
    xb = xb_ref[0]
    mixed = None
    for k, y_ref in enumerate((y1_ref, y2_ref, y3_ref)):
        cs = slice(k * D_MODEL, (k + 1) * D_MODEL)
        gate = _sigmoid(_dot(xb, wg_ref[:, cs]) + bg_ref[:, cs])
        term = gate * _dot(y_ref[0], wbr_ref[k])
        mixed = term if mixed is None else mixed + term
    r = ALPHA * x_ref[0] + _dot(mixed.astype(BF16), wo_ref[...])
    y = _layer_norm(r, g_ref[...], b_ref[...])
    o_ref[0] = y
    ob_ref[0] = y.astype(BF16)


def _merge_call(x, xb, y1, y2, y3, wg, bg, wbr, wo, g, b):
    bsz, L, D = x.shape
    W = BRANCH_WIDTH
    return pl.pallas_call(
        _merge_kernel,
        grid=(bsz, L // ROW_TILE),
        in_specs=[_row_spec(D), _row_spec(D), _row_spec(W), _row_spec(W), _row_spec(W),
                  _const_spec((D, N_BRANCH * D)), _const_spec((1, N_BRANCH * D)),
                  _const_spec((N_BRANCH, W, D)), _const_spec((D, D)),
                  _const_spec((1, D)), _const_spec((1, D))],
        out_specs=[_row_spec(D), _row_spec(D)],
        out_shape=[jax.ShapeDtypeStruct((bsz, L, D), F32), jax.ShapeDtypeStruct((bsz, L, D), BF16)],
        compiler_params=_params(2),
        name="merge",
    )(x, xb, y1, y2, y3, wg, bg, wbr, wo, g, b)


def _ffn_kernel(x_ref, xb_ref, wg_ref, wu_ref, wd_ref, g_ref, b_ref, o_ref, ob_ref):
    xb = xb_ref[0]
    acc = None
    for c in range(FFN_HIDDEN // FFN_CHUNK):
        cs = slice(c * FFN_CHUNK, (c + 1) * FFN_CHUNK)
        gt = _dot(xb, wg_ref[:, cs])
        hid = (gt * _sigmoid(gt)) * _dot(xb, wu_ref[:, cs])
        term = _dot(hid.astype(BF16), wd_ref[cs, :])
        acc = term if acc is None else acc + term
    y = _layer_norm(ALPHA * x_ref[0] + acc, g_ref[...], b_ref[...])
    o_ref[0] = y
    ob_ref[0] = y.astype(BF16)


def _ffn_call(x, xb, wg, wu, wd, g, b):
    bsz, L, D = x.shape
    Hd = FFN_HIDDEN
    once = pl.Buffered(1)
    return pl.pallas_call(
        _ffn_kernel,
        grid=(bsz, L // ROW_TILE),
        in_specs=[_row_spec(D), _row_spec(D),
                  pl.BlockSpec((D, Hd), lambda b, i: (0, 0), pipeline_mode=once),
                  pl.BlockSpec((D, Hd), lambda b, i: (0, 0), pipeline_mode=once),
                  pl.BlockSpec((Hd, D), lambda b, i: (0, 0), pipeline_mode=once),
                  _const_spec((1, D)), _const_spec((1, D))],
        out_specs=[_row_spec(D), _row_spec(D)],
        out_shape=[jax.ShapeDtypeStruct((bsz, L, D), F32), jax.ShapeDtypeStruct((bsz, L, D), BF16)],
        compiler_params=_params(2),
        name="ffn",
    )(x, xb, wg, wu, wd, g, b)


def _s5_weights(a_re, a_im, log_dt, b_re, b_im, c_re, c_im):
    dt = jnp.exp(log_dt)[:, None]
    mag = jnp.exp(a_re * dt)
    lbr = mag * jnp.cos(a_im * dt)
    lbi = mag * jnp.sin(a_im * dt)
    den = a_re * a_re + a_im * a_im
    kr = ((lbr - 1.0) * a_re + lbi * a_im) / den
    ki = (lbi * a_re - (lbr - 1.0) * a_im) / den
    bbr = kr[:, :, None] * b_re - ki[:, :, None] * b_im
    bbi = kr[:, :, None] * b_im + ki[:, :, None] * b_re
    eye = jnp.eye(S5_SLAB_GROUPS, dtype=F32)

    def in_w(part):
        part = part.reshape(S5_SLABS, S5_SLAB_GROUPS, S5_STATE, S5_GROUP)
        return jnp.einsum('bgpc,gh->bgchp', part, eye).reshape(S5_SLABS, S5_SLAB_IN, S5_HALF)

    def out_w(part):
        part = part.reshape(S5_SLABS, S5_SLAB_GROUPS, S5_GROUP, S5_STATE)
        return jnp.einsum('bgcp,gh->bhpgc', part, eye).reshape(S5_SLABS, S5_HALF, S5_SLAB_IN)

    wb = jnp.concatenate([in_w(bbr), in_w(bbi)], axis=2).astype(BF16)
    wc = jnp.concatenate([out_w(c_re), -out_w(c_im)], axis=1).astype(BF16)
    lr = jnp.broadcast_to(lbr.reshape(1, S5_NSTATE), (SUBLANES, S5_NSTATE))
    li = jnp.broadcast_to(lbi.reshape(1, S5_NSTATE), (SUBLANES, S5_NSTATE))
    return wb, wc, lr, li


def _interleave_perm(tc):
    S = tc // SUBLANES
    p = np.zeros((tc, tc), np.float32)
    for c in range(SUBLANES):
        for t in range(S):
            p[SUBLANES * t + c, c * S + t] = 1.0
    return jnp.asarray(p, BF16), jnp.asarray(p.T, BF16)


def _fox_selectors():
    n = FOX_HEADS * LANES
    selq = np.zeros((LANES, n), np.float32)
    selk = np.zeros((LANES, n), np.float32)
    oneq = np.zeros((1, n), np.float32)
    onek = np.zeros((1, n), np.float32)
    for h in range(FOX_HEADS):
        base = h * LANES + FOX_HEAD_DIM
        for part in range(3):
            selq[part * FOX_HEADS + h, base + part] = 1.0
            selk[part * FOX_HEADS + h, base + 3 + part] = -1.0
            oneq[0, base + 3 + part] = 1.0
            onek[0, base + part] = 1.0
    return (jnp.asarray(selq, BF16), jnp.asarray(selk, BF16), jnp.asarray(oneq), jnp.asarray(onek))


def _block_diag(w):
    n, d, _ = w.shape
    return jnp.einsum('hij,hg->higj', w, jnp.eye(n, dtype=w.dtype)).reshape(n * d, n * d)


def _layer_weights(l, p, perm):
    W = BRANCH_WIDTH
    tc = perm.shape[0]
    o_lx, o_q, o_f, o_g = W, 3 * W, 6 * W, 6 * W + FOX_HEADS
    wl = p["w_in"][l]
    wb, wc, lr, li = _s5_weights(p["s5_a_re"][l], p["s5_a_im"][l], p["s5_log_dt"][l], p["s5_b_re"][l],
                                 p["s5_b_im"][l], p["s5_c_re"][l], p["s5_c_im"][l])
    s5_args = (wl[:, 0:o_lx].astype(BF16), wb, wc, lr, li, p["s5_d"][l].reshape(1, W),
               p["s5_w_glu"][l].astype(BF16), p["s5_b_glu"][l].reshape(1, W))
    wax = jnp.concatenate([_block_diag(p["lru_w_a"][l]), _block_diag(p["lru_w_x"][l])],
                          axis=1).astype(BF16)
    bax = jnp.concatenate([p["lru_b_a"][l].reshape(1, W), p["lru_b_x"][l].reshape(1, W)], axis=1)
    clam = (-LRU_C * jax.nn.softplus(-p["lru_lambda"][l])).reshape(1, W)
    lru_args = (wl[:, o_lx:o_q].astype(BF16), p["lru_conv_w"][l], p["lru_conv_b"][l].reshape(1, W),
                wax, bax, clam)
    w_qk = jnp.concatenate([wl[:, o_q:o_q + W] * (LOG2E * FOX_HEAD_DIM ** -0.5),
                            wl[:, o_q + W:o_q + 2 * W]], axis=1).astype(BF16)
    w_vt = wl[:, o_q + 2 * W:o_f].T.astype(BF16)
    w_f = jnp.pad(wl[:, o_f:o_g], ((0, 0), (0, LANES - FOX_HEADS))).astype(BF16)
    bf = jnp.pad(p["b_f"][l], (0, LANES - FOX_HEADS)).reshape(1, LANES)
    tri = (lax.broadcasted_iota(jnp.int32, (tc, tc), 1)
           <= lax.broadcasted_iota(jnp.int32, (tc, tc), 0)).astype(BF16)
    qkv_args = (w_qk, w_vt, w_f, bf, tri) + _fox_selectors()
    return s5_args, lru_args, qkv_args


def kernel(x, w_in, b_f, b_gate, s5_a_re, s5_a_im, s5_log_dt, s5_b_re, s5_b_im, s5_c_re, s5_c_im,
           s5_d, s5_w_glu, s5_b_glu, lru_conv_w, lru_conv_b, lru_w_a, lru_b_a, lru_w_x, lru_b_x,
           lru_lambda, w_branch, w_out, ln1_g, ln1_b, w_ffn_gate, w_ffn_up, w_ffn_down, ln2_g, ln2_b):
    bsz, L, D = x.shape
    p = dict(w_in=w_in, b_f=b_f, s5_a_re=s5_a_re, s5_a_im=s5_a_im, s5_log_dt=s5_log_dt,
             s5_b_re=s5_b_re, s5_b_im=s5_b_im, s5_c_re=s5_c_re, s5_c_im=s5_c_im, s5_d=s5_d,
             s5_w_glu=s5_w_glu, s5_b_glu=s5_b_glu, lru_conv_w=lru_conv_w, lru_conv_b=lru_conv_b,
             lru_w_a=lru_w_a, lru_b_a=lru_b_a, lru_w_x=lru_w_x, lru_b_x=lru_b_x,
             lru_lambda=lru_lambda)
    perm, permt = _interleave_perm(ROW_TILE)
    xb = x.astype(BF16)
    for l in range(DEPTH):
        s5_args, lru_args, qkv_args = _layer_weights(l, p, perm)
        y_s5, y_lru, qa, ka, vt = _branch_call(xb, perm, permt, s5_args, lru_args, qkv_args)
        y_fox = _fox_call(qa, ka, vt)
        w_gate = w_in[l][:, 6 * BRANCH_WIDTH + FOX_HEADS:].astype(BF16)
        x, xb = _merge_call(x, xb, y_s5, y_lru, y_fox, w_gate, b_gate[l].reshape(1, N_BRANCH * D),
                            w_branch[l].astype(BF16), w_out[l].astype(BF16),
                            ln1_g[l].reshape(1, D), ln1_b[l].reshape(1, D))
        x, xb = _ffn_call(x, xb, w_ffn_gate[l].astype(BF16), w_ffn_up[l].astype(BF16),
                          w_ffn_down[l].astype(BF16), ln2_g[l].reshape(1, D), ln2_b[l].reshape(1, D))
    return x
```

```python
import math

import jax
import jax.numpy as jnp
import numpy as np
from jax import lax
from jax.experimental import pallas as pl
from jax.experimental.pallas import tpu as pltpu

F32 = jnp.float32
BF16 = jnp.bfloat16

D_MODEL = 1024
DEPTH = 4
BRANCH_WIDTH = D_MODEL // 2
N_BRANCH = 3
S5_GROUP = 16
S5_GROUPS = BRANCH_WIDTH // S5_GROUP
S5_STATE = 64
LRU_HEADS = 8
LRU_HEAD_DIM = BRANCH_WIDTH // LRU_HEADS
LRU_C = 8.0
CONV_WIDTH = 4
FOX_HEAD_DIM = 64
FOX_HEADS = BRANCH_WIDTH // FOX_HEAD_DIM
FOX_ACC_ROWS = FOX_HEAD_DIM + 16
LOG2E = math.log2(math.e)
FFN_HIDDEN = ((8 * D_MODEL + 3 * 256 - 1) // (3 * 256)) * 256
ALPHA = (2.0 * DEPTH) ** 0.25
LN_EPS = 1e-5

LANES = 128
SUBLANES = 8
S5_SLABS = 4
S5_SLAB_GROUPS = S5_GROUPS // S5_SLABS
S5_HALF = S5_SLAB_GROUPS * S5_STATE
S5_SLAB_IN = S5_SLAB_GROUPS * S5_GROUP
S5_NSTATE = S5_GROUPS * S5_STATE

ROW_TILE = 512
ATT_TILE = 256
FFN_CHUNK = 1408
VMEM_LIMIT = 56 * 1024 * 1024


def _params(n_axes):
    return pltpu.CompilerParams(
        dimension_semantics=("arbitrary",) * n_axes, vmem_limit_bytes=VMEM_LIMIT)


def _dot(a, b):
    return jnp.dot(a, b, preferred_element_type=F32)


def _sigmoid(x):
    return 1.0 / (1.0 + jnp.exp(-x))


def _gelu(x):
    return 0.5 * x * (1.0 + jnp.tanh(math.sqrt(2.0 / math.pi) * (x + 0.044715 * (x * x * x))))


def _layer_norm(r, g, b):
    mu = jnp.mean(r, axis=-1, keepdims=True)
    d = r - mu
    var = jnp.mean(d * d, axis=-1, keepdims=True)
    return d * lax.rsqrt(var + LN_EPS) * g + b


def _layer_spec(arr, l):
    nd = arr.ndim - 1
    return pl.BlockSpec((None,) + arr.shape[1:], lambda b, i: (l,) + (0,) * nd,
                        pipeline_mode=pl.Buffered(1))


def _shared_spec(arr):
    nd = arr.ndim
    return pl.BlockSpec(arr.shape, lambda b, i: (0,) * nd, pipeline_mode=pl.Buffered(1))


def _row_spec(width, tile=ROW_TILE):
    return pl.BlockSpec((1, tile, width), lambda b, i: (b, i, 0))


def _split3(x):
    hi = x.astype(BF16)
    r1 = x - hi.astype(F32)
    mid = r1.astype(BF16)
    lo = (r1 - mid.astype(F32)).astype(BF16)
    return hi, mid, lo


def _cmul(ar, ai, br, bi):
    return ar * br - ai * bi, ar * bi + ai * br


def _s5_scan(bu_scr, hb_scr, l_scr, e_scr, carry_scr, lr_ref, li_ref, S):
    H = S5_HALF
    nq = H // LANES
    pair = 2
    for b0 in range(0, S5_SLABS, pair):
        cols = []
        for blk in range(b0, b0 + pair):
            for q in range(nq):
                c0 = blk * 2 * H + q * LANES
                l0 = blk * H + q * LANES
                cols.append((slice(c0, c0 + LANES), slice(c0 + H, c0 + H + LANES),
                             slice(l0, l0 + LANES)))
        n = len(cols)
        lr = [lr_ref[:, cl] for _, _, cl in cols]
        li = [li_ref[:, cl] for _, _, cl in cols]
        hr = [jnp.zeros((SUBLANES, LANES), F32)] * n
        hi = [jnp.zeros((SUBLANES, LANES), F32)] * n
        for t in range(S):
            rows = slice(t * SUBLANES, (t + 1) * SUBLANES)
            for k, (cr, ci, _) in enumerate(cols):
                pr, pi = _cmul(lr[k], li[k], hr[k], hi[k])
                hr[k] = pr + bu_scr[rows, cr]
                hi[k] = pi + bu_scr[rows, ci]
                bu_scr[rows, cr] = hr[k]
                bu_scr[rows, ci] = hi[k]
        for k, (cr, ci, _) in enumerate(cols):
            psr, psi = lr[k], li[k]
            for _ in range(S.bit_length() - 1):
                psr, psi = _cmul(psr, psi, psr, psi)
            psr, psi = psr[0:1, :], psi[0:1, :]
            l_scr[:, cr] = hr[k]
            l_scr[:, ci] = hi[k]
            er = carry_scr[:, cr]
            ei = carry_scr[:, ci]
            for c in range(SUBLANES):
                e_scr[c:c + 1, cr] = er
                e_scr[c:c + 1, ci] = ei
                pr, pi = _cmul(psr, psi, er, ei)
                er = l_scr[c:c + 1, cr] + pr
                ei = l_scr[c:c + 1, ci] + pi
            carry_scr[:, cr] = er
            carry_scr[:, ci] = ei
        gr = [e_scr[:, cr] for cr, _, _ in cols]
        gi = [e_scr[:, ci] for _, ci, _ in cols]
        for t2 in range(S // 2):
            rows2 = slice(2 * t2 * SUBLANES, (2 * t2 + 2) * SUBLANES)
            for k, (cr, ci, _) in enumerate(cols):
                outr, outi = [], []
                for t in (2 * t2, 2 * t2 + 1):
                    rows = slice(t * SUBLANES, (t + 1) * SUBLANES)
                    gr[k], gi[k] = _cmul(lr[k], li[k], gr[k], gi[k])
                    outr.append(bu_scr[rows, cr] + gr[k])
                    outi.append(bu_scr[rows, ci] + gi[k])
                hb_scr[rows2, cr] = jnp.concatenate(outr, axis=0).astype(BF16)
                hb_scr[rows2, ci] = jnp.concatenate(outi, axis=0).astype(BF16)


def _branch_kernel(xb_ref, perm_ref, permt_ref,
                   s_win_ref, s_wb_ref, s_wc_ref, s_lr_ref, s_li_ref, s_d_ref, s_wg_ref, s_bg_ref,
                   r_win_ref, r_cw_ref, r_cb_ref, r_wax_ref, r_bax_ref, r_clam_ref,
                   wqk_ref, wvt_ref, wf_ref, bf_ref, tri_ref, selq_ref, selk_ref, oneq_ref, onek_ref,
                   ys_ref, yr_ref, qa_ref, ka_ref, vt_ref,
                   bu_scr, hb_scr, sl_scr, se_scr, scarry_scr, u_scr,
                   xl_scr, a_scr, b_scr, g_scr, rl_scr, re_scr, rcarry_scr, halo_scr, fcarry_scr):
    tc = xb_ref.shape[1]
    S = tc // SUBLANES
    W = BRANCH_WIDTH
    H = S5_HALF
    nback = CONV_WIDTH - 1
    pre = nback * SUBLANES

    @pl.when(pl.program_id(1) == 0)
    def _():
        scarry_scr[...] = jnp.zeros_like(scarry_scr)
        rcarry_scr[...] = jnp.zeros_like(rcarry_scr)
        halo_scr[...] = jnp.zeros_like(halo_scr)
        fcarry_scr[...] = jnp.zeros_like(fcarry_scr)

    x = xb_ref[0]
    xp = _dot(perm_ref[...], x).astype(BF16)

    u = _dot(xp, s_win_ref[...])
    u_scr[...] = u
    ub = u.astype(BF16)
    for blk in range(S5_SLABS):
        bu_scr[:, blk * 2 * H:(blk + 1) * 2 * H] = _dot(
            ub[:, blk * S5_SLAB_IN:(blk + 1) * S5_SLAB_IN], s_wb_ref[blk])

    z = _dot(xp, r_win_ref[...])
    xl = z[:, 0:W]
    g_scr[...] = z[:, W:2 * W]
    xl_scr[pre:pre + tc, :] = xl
    first = lax.broadcasted_iota(jnp.int32, (SUBLANES, W), 0) == 0
    for k in range(1, CONV_WIDTH):
        shifted = pltpu.roll(xl[(S - k) * SUBLANES:(S - k + 1) * SUBLANES, :], 1, 0)
        hrows = slice((k - 1) * SUBLANES, k * SUBLANES)
        xl_scr[(nback - k) * SUBLANES:(nback - k + 1) * SUBLANES, :] = jnp.where(
            first, halo_scr[hrows, :], shifted)
        halo_scr[hrows, :] = shifted
    xc = r_cb_ref[...] + r_cw_ref[nback:nback + 1, :] * xl
    for k in range(1, CONV_WIDTH):
        back = pre - k * SUBLANES
        xc = xc + r_cw_ref[nback - k:nback - k + 1, :] * xl_scr[back:back + tc, :]
    ra = _dot(xc.astype(BF16), r_wax_ref[...]) + r_bax_ref[...]
    r = _sigmoid(ra[:, 0:W])
    ig = _sigmoid(ra[:, W:2 * W])
    log_a = r_clam_ref[...] * r
    a = jnp.exp(log_a)
    mult = jnp.sqrt(-jnp.tanh(log_a) * (a * a + 1.0))
    a_scr[...] = a
    b_scr[...] = mult * (ig * xc)

    f = _dot(x, wf_ref[...]) + bf_ref[...]
    lf = jnp.minimum(f, 0.0) - jnp.log1p(jnp.exp(-jnp.abs(f)))
    hi, mid, lo = _split3(lf)
    tri = tri_ref[...]
    cum = _dot(tri, hi) + _dot(tri, mid) + _dot(tri, lo) + fcarry_scr[...]
    fcarry_scr[...] = cum[tc - 1:tc, :]
    c_hi, c_mid, c_lo = _split3(cum * LOG2E)
    lane = lax.broadcasted_iota(jnp.int32, cum.shape, 1)
    h8 = FOX_HEADS
    c3 = jnp.where(lane < h8, c_hi.astype(F32),
                   jnp.where(lane < 2 * h8, pltpu.roll(c_mid.astype(F32), h8, 1),
                             jnp.where(lane < 3 * h8, pltpu.roll(c_lo.astype(F32), 2 * h8, 1), 0.0)))
    c3 = c3.astype(BF16)
    low = lax.broadcasted_iota(jnp.int32, (tc, LANES), 1) < FOX_HEAD_DIM

    def augment(zz, sel_ref, one_ref, out_ref):
        extra = _dot(c3, sel_ref[...]) + one_ref[...]
        for hp in range(FOX_HEADS // 2):
            two = zz[:, hp * LANES:(hp + 1) * LANES]
            for odd in range(2):
                h = 2 * hp + odd
                src = pltpu.roll(two, FOX_HEAD_DIM, 1) if odd else two
                out_ref[0, :, h * LANES:(h + 1) * LANES] = jnp.where(
                    low, src, extra[:, h * LANES:(h + 1) * LANES]).astype(BF16)

    augment(_dot(x, wqk_ref[:, 0:W]), selq_ref, oneq_ref, qa_ref)
    augment(_dot(x, wqk_ref[:, W:2 * W]), selk_ref, onek_ref, ka_ref)
    nt = (((1,), (1,)), ((), ()))
    vt_ref[0] = lax.dot_general(wvt_ref[...], x, nt, preferred_element_type=F32).astype(BF16)

    _s5_scan(bu_scr, hb_scr, sl_scr, se_scr, scarry_scr, s_lr_ref, s_li_ref, S)
    h = jnp.zeros((SUBLANES, W), F32)
    ap = jnp.ones((SUBLANES, W), F32)
    for t in range(S):
        rows = slice(t * SUBLANES, (t + 1) * SUBLANES)
        at = a_scr[rows, :]
        h = at * h + b_scr[rows, :]
        ap = at * ap
        b_scr[rows, :] = h
        a_scr[rows, :] = ap
    rl_scr[0:SUBLANES, :] = h
    rl_scr[SUBLANES:2 * SUBLANES, :] = ap
    e = rcarry_scr[...]
    for c in range(SUBLANES):
        re_scr[c:c + 1, :] = e
        e = rl_scr[c:c + 1, :] + rl_scr[SUBLANES + c:SUBLANES + c + 1, :] * e
    rcarry_scr[...] = e

    ys = [_dot(hb_scr[:, blk * 2 * H:(blk + 1) * 2 * H], s_wc_ref[blk]) for blk in range(S5_SLABS)]
    y = jnp.concatenate(ys, axis=1) + s_d_ref[...] * u_scr[...]
    y = _gelu(y)
    y = y * _sigmoid(_dot(y.astype(BF16), s_wg_ref[...]) + s_bg_ref[...])
    ys_ref[0] = _dot(permt_ref[...], y.astype(BF16)).astype(BF16)
    et = re_scr[...][None]
    hs = b_scr[...].reshape(S, SUBLANES, W) + a_scr[...].reshape(S, SUBLANES, W) * et
    yr = (_gelu(g_scr[...]) * hs.reshape(tc, W)).astype(BF16)
    yr_ref[0] = _dot(permt_ref[...], yr).astype(BF16)


def _branch_call(xb, l, perm, permt, layered, shared):
    bsz, L, D = xb.shape
    tc = ROW_TILE
    W = BRANCH_WIDTH
    n = FOX_HEADS * LANES
    ncol = 2 * S5_NSTATE
    nback = CONV_WIDTH - 1
    args = (xb, perm, permt) + tuple(layered) + tuple(shared)
    in_specs = ([_row_spec(D), _shared_spec(perm), _shared_spec(permt)]
                + [_layer_spec(a, l) for a in layered] + [_shared_spec(a) for a in shared])
    f32 = lambda *s: pltpu.VMEM(s, F32)
    return pl.pallas_call(
        _branch_kernel,
        grid=(bsz, L // tc),
        in_specs=in_specs,
        out_specs=[_row_spec(W), _row_spec(W), _row_spec(n), _row_spec(n),
                   pl.BlockSpec((1, W, tc), lambda b, i: (b, 0, i))],
        out_shape=[jax.ShapeDtypeStruct((bsz, L, W), BF16), jax.ShapeDtypeStruct((bsz, L, W), BF16),
                   jax.ShapeDtypeStruct((bsz, L, n), BF16), jax.ShapeDtypeStruct((bsz, L, n), BF16),
                   jax.ShapeDtypeStruct((bsz, W, L), BF16)],
        scratch_shapes=[f32(tc, ncol), pltpu.VMEM((tc, ncol), BF16), f32(SUBLANES, ncol),
                        f32(SUBLANES, ncol), f32(1, ncol), f32(tc, W),
                        f32(tc + nback * SUBLANES, W), f32(tc, W), f32(tc, W), f32(tc, W),
                        f32(2 * SUBLANES, W), f32(SUBLANES, W), f32(1, W), f32(nback * SUBLANES, W),
                        f32(1, LANES)],
        compiler_params=_params(2),
        name="branch",
    )(*args)


def _fox_kernel(qa_ref, ka_ref, vt_ref, o_ref, acc_scr, ot_scr, m_scr, st_scr, mx_scr):
    T = qa_ref.shape[1]
    i = pl.program_id(1)
    dh = FOX_HEAD_DIM
    nt = (((1,), (1,)), ((), ()))
    key = lax.broadcasted_iota(jnp.int32, (T, T), 0)
    qry = lax.broadcasted_iota(jnp.int32, (T, T), 1)
    causal = key <= qry
    ones = jnp.ones((FOX_ACC_ROWS - dh, T), BF16)

    m_scr[...] = jnp.full(m_scr.shape, -jnp.inf, F32)
    acc_scr[...] = jnp.zeros(acc_scr.shape, F32)

    def scores(j, slot, masked):
        r0 = pl.multiple_of(j * T, T)
        for h in range(FOX_HEADS):
            hl = slice(h * LANES, (h + 1) * LANES)
            st = lax.dot_general(ka_ref[0, pl.ds(r0, T), hl], qa_ref[0, :, hl], nt,
                                 preferred_element_type=F32)
            if masked:
                st = jnp.where(causal, st, -jnp.inf)
            st_scr[slot, h] = st
            mx_scr[slot, h:h + 1, :] = jnp.max(st, axis=0, keepdims=True)

    def consume(j, slot):
        r0 = pl.multiple_of(j * T, T)
        m_all = m_scr[...]
        mx_all = mx_scr[slot]
        m_rows = []
        for h in range(FOX_HEADS):
            ha = slice(h * FOX_ACC_ROWS, (h + 1) * FOX_ACC_ROWS)
            m = m_all[h:h + 1, :]
            m_new = jnp.maximum(m, mx_all[h:h + 1, :])
            alpha = jnp.exp2(m - m_new)
            pt = jnp.exp2(st_scr[slot, h] - m_new).astype(BF16)
            m_rows.append(m_new)
            v1 = jnp.concatenate([vt_ref[0, h * dh:(h + 1) * dh, pl.ds(r0, T)], ones], axis=0)
            acc_scr[ha, :] = alpha * acc_scr[ha, :] + _dot(v1, pt)
        m_scr[...] = jnp.concatenate(m_rows, axis=0)

    npairs = jnp.maximum(i - 1, 0) // 2
    tail = 2 * npairs

    @pl.when(i > 0)
    def _():
        scores(0, 0, False)

    def body(p, carry):
        j = 2 * p
        scores(j + 1, 1, False)
        consume(j, 0)
        scores(j + 2, 0, False)
        consume(j + 1, 1)
        return carry

    lax.fori_loop(0, npairs, body, 0)

    @pl.when(i == 0)
    def _():
        scores(0, 0, True)
        consume(0, 0)

    @pl.when(i % 2 == 1)
    def _():
        scores(i, 1, True)
        consume(tail, 0)
        consume(i, 1)

    @pl.when(jnp.logical_and(i > 0, i % 2 == 0))
    def _():
        scores(tail + 1, 1, False)
        consume(tail, 0)
        scores(i, 0, True)
        consume(tail + 1, 1)
        consume(i, 0)
    for h in range(FOX_HEADS):
        a0 = h * FOX_ACC_ROWS
        ot_scr[h * dh:(h + 1) * dh, :] = acc_scr[a0:a0 + dh, :] / acc_scr[a0 + dh:a0 + dh + 1, :]
    o_ref[0] = ot_scr[...].T.astype(BF16)


def _fox_call(qa, ka, vt):
    bsz, L, n = qa.shape
    W = BRANCH_WIDTH
    T = ATT_TILE
    return pl.pallas_call(
        _fox_kernel,
        grid=(bsz, L // T),
        in_specs=[_row_spec(n, T),
                  pl.BlockSpec((1, L, n), lambda b, i: (b, 0, 0)),
                  pl.BlockSpec((1, W, L), lambda b, i: (b, 0, 0))],
        out_specs=_row_spec(W, T),
        out_shape=jax.ShapeDtypeStruct((bsz, L, W), BF16),
        scratch_shapes=[pltpu.VMEM((FOX_HEADS * FOX_ACC_ROWS, T), F32), pltpu.VMEM((W, T), F32),
                        pltpu.VMEM((FOX_HEADS, T), F32), pltpu.VMEM((2, FOX_HEADS, T, T), F32),
                        pltpu.VMEM((2, FOX_HEADS, T), F32)],
        compiler_params=_params(2),
        name="fox",
    )(qa, ka, vt)


def _merge_kernel(x_ref, xb_ref, y1_ref, y2_ref, y3_ref, wg_ref, bg_ref, wbr_ref, wo_ref,
                  g_ref, b_ref, o_ref, ob_ref):
    xb = xb_ref[0]
    mixed = None
    for k, y_ref in enumerate((y1_ref, y2_ref, y3_ref)):
        cs = slice(k * D_MODEL, (k + 1) * D_MODEL)
        gate = _sigmoid(_dot(xb, wg_ref[:, cs]) + bg_ref[:, cs])
        term = gate * _dot(y_ref[0], wbr_ref[k])
        mixed = term if mixed is None else mixed + term
    r = ALPHA * x_ref[0] + _dot(mixed.astype(BF16), wo_ref[...])
    y = _layer_norm(r, g_ref[...], b_ref[...])
    o_ref[0] = y
    ob_ref[0] = y.astype(BF16)


def _merge_call(x, xb, y1, y2, y3, l, layered):
    bsz, L, D = x.shape
    W = BRANCH_WIDTH
    return pl.pallas_call(
        _merge_kernel,
        grid=(bsz, L // ROW_TILE),
        in_specs=[_row_spec(D), _row_spec(D), _row_spec(W), _row_spec(W), _row_spec(W)]
        + [_layer_spec(a, l) for a in layered],
        out_specs=[_row_spec(D), _row_spec(D)],
        out_shape=[jax.ShapeDtypeStruct((bsz, L, D), F32), jax.ShapeDtypeStruct((bsz, L, D), BF16)],
        compiler_params=_params(2),
        name="merge",
    )(x, xb, y1, y2, y3, *layered)


def _ffn_kernel(x_ref, xb_ref, wg_ref, wu_ref, wd_ref, g_ref, b_ref, o_ref, ob_ref):
    xb = xb_ref[0]
    acc = None
    for c in range(FFN_HIDDEN // FFN_CHUNK):
        cs = slice(c * FFN_CHUNK, (c + 1) * FFN_CHUNK)
        gt = _dot(xb, wg_ref[:, cs])
        hid = (gt * _sigmoid(gt)) * _dot(xb, wu_ref[:, cs])
        term = _dot(hid.astype(BF16), wd_ref[cs, :])
        acc = term if acc is None else acc + term
    y = _layer_norm(ALPHA * x_ref[0] + acc, g_ref[...], b_ref[...])
    o_ref[0] = y
    ob_ref[0] = y.astype(BF16)


def _ffn_call(x, xb, l, layered):
    bsz, L, D = x.shape
    return pl.pallas_call(
        _ffn_kernel,
        grid=(bsz, L // ROW_TILE),
        in_specs=[_row_spec(D), _row_spec(D)] + [_layer_spec(a, l) for a in layered],
        out_specs=[_row_spec(D), _row_spec(D)],
        out_shape=[jax.ShapeDtypeStruct((bsz, L, D), F32), jax.ShapeDtypeStruct((bsz, L, D), BF16)],
        compiler_params=_params(2),
        name="ffn",
    )(x, xb, *layered)


def _s5_weights(a_re, a_im, log_dt, b_re, b_im, c_re, c_im):
    dt = jnp.exp(log_dt)[:, None]
    mag = jnp.exp(a_re * dt)
    lbr = mag * jnp.cos(a_im * dt)
    lbi = mag * jnp.sin(a_im * dt)
    den = a_re * a_re + a_im * a_im
    kr = ((lbr - 1.0) * a_re + lbi * a_im) / den
    ki = (lbi * a_re - (lbr - 1.0) * a_im) / den
    bbr = kr[:, :, None] * b_re - ki[:, :, None] * b_im
    bbi = kr[:, :, None] * b_im + ki[:, :, None] * b_re
    eye = jnp.eye(S5_SLAB_GROUPS, dtype=F32)

    def in_w(part):
        part = part.reshape(S5_SLABS, S5_SLAB_GROUPS, S5_STATE, S5_GROUP)
        return jnp.einsum('bgpc,gh->bgchp', part, eye).reshape(S5_SLABS, S5_SLAB_IN, S5_HALF)

    def out_w(part):
        part = part.reshape(S5_SLABS, S5_SLAB_GROUPS, S5_GROUP, S5_STATE)
        return jnp.einsum('bgcp,gh->bhpgc', part, eye).reshape(S5_SLABS, S5_HALF, S5_SLAB_IN)

    wb = jnp.concatenate([in_w(bbr), in_w(bbi)], axis=2).astype(BF16)
    wc = jnp.concatenate([out_w(c_re), -out_w(c_im)], axis=1).astype(BF16)
    lr = jnp.broadcast_to(lbr.reshape(1, S5_NSTATE), (SUBLANES, S5_NSTATE))
    li = jnp.broadcast_to(lbi.reshape(1, S5_NSTATE), (SUBLANES, S5_NSTATE))
    return wb, wc, lr, li


def _interleave_perm(tc):
    S = tc // SUBLANES
    p = np.zeros((tc, tc), np.float32)
    for c in range(SUBLANES):
        for t in range(S):
            p[SUBLANES * t + c, c * S + t] = 1.0
    return jnp.asarray(p, BF16), jnp.asarray(p.T, BF16)


def _fox_selectors():
    n = FOX_HEADS * LANES
    selq = np.zeros((LANES, n), np.float32)
    selk = np.zeros((LANES, n), np.float32)
    oneq = np.zeros((1, n), np.float32)
    onek = np.zeros((1, n), np.float32)
    for h in range(FOX_HEADS):
        base = h * LANES + FOX_HEAD_DIM
        for part in range(3):
            selq[part * FOX_HEADS + h, base + part] = 1.0
            selk[part * FOX_HEADS + h, base + 3 + part] = -1.0
            oneq[0, base + 3 + part] = 1.0
            onek[0, base + part] = 1.0
    return (jnp.asarray(selq, BF16), jnp.asarray(selk, BF16), jnp.asarray(oneq), jnp.asarray(onek))


def _block_diag(w):
    n, d, _ = w.shape
    return jnp.einsum('hij,hg->higj', w, jnp.eye(n, dtype=w.dtype)).reshape(n * d, n * d)


def _branch_weights(p):
    W = BRANCH_WIDTH
    nl = p["w_in"].shape[0]
    o_lx, o_q, o_f, o_g = W, 3 * W, 6 * W, 6 * W + FOX_HEADS
    w_in = p["w_in"]
    row = lambda a: a.reshape(nl, 1, -1)
    wb, wc, lr, li = jax.vmap(_s5_weights)(p["s5_a_re"], p["s5_a_im"], p["s5_log_dt"], p["s5_b_re"],
                                           p["s5_b_im"], p["s5_c_re"], p["s5_c_im"])
    s5 = (w_in[:, :, 0:o_lx].astype(BF16), wb, wc, lr, li, row(p["s5_d"]),
          p["s5_w_glu"].astype(BF16), row(p["s5_b_glu"]))
    bd = jax.vmap(_block_diag)
    wax = jnp.concatenate([bd(p["lru_w_a"]), bd(p["lru_w_x"])], axis=2).astype(BF16)
    bax = jnp.concatenate([row(p["lru_b_a"]), row(p["lru_b_x"])], axis=2)
    clam = row(-LRU_C * jax.nn.softplus(-p["lru_lambda"]))
    lru = (w_in[:, :, o_lx:o_q].astype(BF16), p["lru_conv_w"], row(p["lru_conv_b"]), wax, bax, clam)
    w_qk = jnp.concatenate([w_in[:, :, o_q:o_q + W] * (LOG2E * FOX_HEAD_DIM ** -0.5),
                            w_in[:, :, o_q + W:o_q + 2 * W]], axis=2).astype(BF16)
    w_vt = jnp.transpose(w_in[:, :, o_q + 2 * W:o_f], (0, 2, 1)).astype(BF16)
    lane_pad = LANES - FOX_HEADS
    w_f = jnp.pad(w_in[:, :, o_f:o_g], ((0, 0), (0, 0), (0, lane_pad))).astype(BF16)
    bf = row(jnp.pad(p["b_f"], ((0, 0), (0, lane_pad))))
    return s5 + lru + (w_qk, w_vt, w_f, bf), w_in[:, :, o_g:].astype(BF16)


def kernel(x, w_in, b_f, b_gate, s5_a_re, s5_a_im, s5_log_dt, s5_b_re, s5_b_im, s5_c_re, s5_c_im,
           s5_d, s5_w_glu, s5_b_glu, lru_conv_w, lru_conv_b, lru_w_a, lru_b_a, lru_w_x, lru_b_x,
           lru_lambda, w_branch, w_out, ln1_g, ln1_b, w_ffn_gate, w_ffn_up, w_ffn_down, ln2_g, ln2_b):
    p = dict(w_in=w_in, b_f=b_f, s5_a_re=s5_a_re, s5_a_im=s5_a_im, s5_log_dt=s5_log_dt,
             s5_b_re=s5_b_re, s5_b_im=s5_b_im, s5_c_re=s5_c_re, s5_c_im=s5_c_im, s5_d=s5_d,
             s5_w_glu=s5_w_glu, s5_b_glu=s5_b_glu, lru_conv_w=lru_conv_w, lru_conv_b=lru_conv_b,
             lru_w_a=lru_w_a, lru_b_a=lru_b_a, lru_w_x=lru_w_x, lru_b_x=lru_b_x,
             lru_lambda=lru_lambda)
    nl = w_in.shape[0]
    row = lambda a: a.reshape(nl, 1, -1)
    perm, permt = _interleave_perm(ROW_TILE)
    tri = (lax.broadcasted_iota(jnp.int32, (ROW_TILE, ROW_TILE), 1)
           <= lax.broadcasted_iota(jnp.int32, (ROW_TILE, ROW_TILE), 0)).astype(BF16)
    shared = (tri,) + _fox_selectors()
    branch_w, w_gate = _branch_weights(p)
    merge_w = (w_gate, row(b_gate), w_branch.astype(BF16), w_out.astype(BF16), row(ln1_g), row(ln1_b))
    ffn_w = (w_ffn_gate.astype(BF16), w_ffn_up.astype(BF16), w_ffn_down.astype(BF16),
             row(ln2_g), row(ln2_b))
    xb = x.astype(BF16)
    for l in range(nl):
        y_s5, y_lru, qa, ka, vt = _branch_call(xb, l, perm, permt, branch_w, shared)
        y_fox = _fox_call(qa, ka, vt)
        x, xb = _merge_call(x, xb, y_s5, y_lru, y_fox, l, merge_w)
        x, xb = _ffn_call(x, xb, l, ffn_w)
    return x
```

```python
import math

import jax
import jax.numpy as jnp
import numpy as np
from jax import lax
from jax.experimental import pallas as pl
from jax.experimental.pallas import tpu as pltpu

F32 = jnp.float32
BF16 = jnp.bfloat16

D_MODEL = 1024
DEPTH = 4
BRANCH_WIDTH = D_MODEL // 2
N_BRANCH = 3
S5_GROUP = 16
S5_GROUPS = BRANCH_WIDTH // S5_GROUP
S5_STATE = 64
LRU_HEADS = 8
LRU_HEAD_DIM = BRANCH_WIDTH // LRU_HEADS
LRU_C = 8.0
CONV_WIDTH = 4
FOX_HEAD_DIM = 64
FOX_HEADS = BRANCH_WIDTH // FOX_HEAD_DIM
FOX_ACC_ROWS = FOX_HEAD_DIM + 16
LOG2E = math.log2(math.e)
FFN_HIDDEN = ((8 * D_MODEL + 3 * 256 - 1) // (3 * 256)) * 256
ALPHA = (2.0 * DEPTH) ** 0.25
LN_EPS = 1e-5

LANES = 128
SUBLANES = 8
S5_SLABS = 4
S5_SLAB_GROUPS = S5_GROUPS // S5_SLABS
S5_HALF = S5_SLAB_GROUPS * S5_STATE
S5_SLAB_IN = S5_SLAB_GROUPS * S5_GROUP
S5_NSTATE = S5_GROUPS * S5_STATE

ROW_TILE = 512
ATT_TILE = 512
FFN_CHUNK = 1408
VMEM_LIMIT = 56 * 1024 * 1024


def _params(n_axes):
    return pltpu.CompilerParams(
        dimension_semantics=("arbitrary",) * n_axes, vmem_limit_bytes=VMEM_LIMIT)


def _dot(a, b):
    return jnp.dot(a, b, preferred_element_type=F32)


def _sigmoid(x):
    return 1.0 / (1.0 + jnp.exp(-x))


def _gelu(x):
    return 0.5 * x * (1.0 + jnp.tanh(math.sqrt(2.0 / math.pi) * (x + 0.044715 * (x * x * x))))


def _layer_norm(r, g, b):
    mu = jnp.mean(r, axis=-1, keepdims=True)
    d = r - mu
    var = jnp.mean(d * d, axis=-1, keepdims=True)
    return d * lax.rsqrt(var + LN_EPS) * g + b


def _layer_spec(arr, l):
    nd = arr.ndim - 1
    return pl.BlockSpec((None,) + arr.shape[1:], lambda b, i: (l,) + (0,) * nd,
                        pipeline_mode=pl.Buffered(1))


def _shared_spec(arr):
    nd = arr.ndim
    return pl.BlockSpec(arr.shape, lambda b, i: (0,) * nd, pipeline_mode=pl.Buffered(1))


def _row_spec(width, tile=ROW_TILE):
    return pl.BlockSpec((1, tile, width), lambda b, i: (b, i, 0))


def _split3(x):
    hi = x.astype(BF16)
    r1 = x - hi.astype(F32)
    mid = r1.astype(BF16)
    lo = (r1 - mid.astype(F32)).astype(BF16)
    return hi, mid, lo


def _cmul(ar, ai, br, bi):
    return ar * br - ai * bi, ar * bi + ai * br


def _s5_scan(bu_scr, hb_scr, l_scr, e_scr, carry_scr, lr_ref, li_ref, S):
    H = S5_HALF
    nq = H // LANES
    pair = 2
    for b0 in range(0, S5_SLABS, pair):
        cols = []
        for blk in range(b0, b0 + pair):
            for q in range(nq):
                c0 = blk * 2 * H + q * LANES
                l0 = blk * H + q * LANES
                cols.append((slice(c0, c0 + LANES), slice(c0 + H, c0 + H + LANES),
                             slice(l0, l0 + LANES)))
        n = len(cols)
        lr = [lr_ref[:, cl] for _, _, cl in cols]
        li = [li_ref[:, cl] for _, _, cl in cols]
        hr = [jnp.zeros((SUBLANES, LANES), F32)] * n
        hi = [jnp.zeros((SUBLANES, LANES), F32)] * n
        for t in range(S):
            rows = slice(t * SUBLANES, (t + 1) * SUBLANES)
            for k, (cr, ci, _) in enumerate(cols):
                pr, pi = _cmul(lr[k], li[k], hr[k], hi[k])
                hr[k] = pr + bu_scr[rows, cr]
                hi[k] = pi + bu_scr[rows, ci]
                bu_scr[rows, cr] = hr[k]
                bu_scr[rows, ci] = hi[k]
        for k, (cr, ci, _) in enumerate(cols):
            psr, psi = lr[k], li[k]
            for _ in range(S.bit_length() - 1):
                psr, psi = _cmul(psr, psi, psr, psi)
            psr, psi = psr[0:1, :], psi[0:1, :]
            l_scr[:, cr] = hr[k]
            l_scr[:, ci] = hi[k]
            er = carry_scr[:, cr]
            ei = carry_scr[:, ci]
            for c in range(SUBLANES):
                e_scr[c:c + 1, cr] = er
                e_scr[c:c + 1, ci] = ei
                pr, pi = _cmul(psr, psi, er, ei)
                er = l_scr[c:c + 1, cr] + pr
                ei = l_scr[c:c + 1, ci] + pi
            carry_scr[:, cr] = er
            carry_scr[:, ci] = ei
        gr = [e_scr[:, cr] for cr, _, _ in cols]
        gi = [e_scr[:, ci] for _, ci, _ in cols]
        for t2 in range(S // 2):
            rows2 = slice(2 * t2 * SUBLANES, (2 * t2 + 2) * SUBLANES)
            for k, (cr, ci, _) in enumerate(cols):
                outr, outi = [], []
                for t in (2 * t2, 2 * t2 + 1):
                    rows = slice(t * SUBLANES, (t + 1) * SUBLANES)
                    gr[k], gi[k] = _cmul(lr[k], li[k], gr[k], gi[k])
                    outr.append(bu_scr[rows, cr] + gr[k])
                    outi.append(bu_scr[rows, ci] + gi[k])
                hb_scr[rows2, cr] = jnp.concatenate(outr, axis=0).astype(BF16)
                hb_scr[rows2, ci] = jnp.concatenate(outi, axis=0).astype(BF16)


def _branch_kernel(xb_ref, perm_ref, permt_ref,
                   s_win_ref, s_wb_ref, s_wc_ref, s_lr_ref, s_li_ref, s_d_ref, s_wg_ref, s_bg_ref,
                   r_win_ref, r_cw_ref, r_cb_ref, r_wax_ref, r_bax_ref, r_clam_ref,
                   wqk_ref, wvt_ref, wf_ref, bf_ref, tri_ref, selq_ref, selk_ref, oneq_ref, onek_ref,
                   ys_ref, yr_ref, qa_ref, ka_ref, vt_ref,
                   bu_scr, hb_scr, sl_scr, se_scr, scarry_scr, u_scr,
                   xl_scr, a_scr, b_scr, g_scr, rl_scr, re_scr, rcarry_scr, halo_scr, fcarry_scr):
    tc = xb_ref.shape[1]
    S = tc // SUBLANES
    W = BRANCH_WIDTH
    H = S5_HALF
    nback = CONV_WIDTH - 1
    pre = nback * SUBLANES

    @pl.when(pl.program_id(1) == 0)
    def _():
        scarry_scr[...] = jnp.zeros_like(scarry_scr)
        rcarry_scr[...] = jnp.zeros_like(rcarry_scr)
        halo_scr[...] = jnp.zeros_like(halo_scr)
        fcarry_scr[...] = jnp.zeros_like(fcarry_scr)

    x = xb_ref[0]
    xp = _dot(perm_ref[...], x).astype(BF16)

    u = _dot(xp, s_win_ref[...])
    u_scr[...] = u
    ub = u.astype(BF16)
    for blk in range(S5_SLABS):
        bu_scr[:, blk * 2 * H:(blk + 1) * 2 * H] = _dot(
            ub[:, blk * S5_SLAB_IN:(blk + 1) * S5_SLAB_IN], s_wb_ref[blk])

    z = _dot(xp, r_win_ref[...])
    xl = z[:, 0:W]
    g_scr[...] = z[:, W:2 * W]
    xl_scr[pre:pre + tc, :] = xl
    first = lax.broadcasted_iota(jnp.int32, (SUBLANES, W), 0) == 0
    for k in range(1, CONV_WIDTH):
        shifted = pltpu.roll(xl[(S - k) * SUBLANES:(S - k + 1) * SUBLANES, :], 1, 0)
        hrows = slice((k - 1) * SUBLANES, k * SUBLANES)
        xl_scr[(nback - k) * SUBLANES:(nback - k + 1) * SUBLANES, :] = jnp.where(
            first, halo_scr[hrows, :], shifted)
        halo_scr[hrows, :] = shifted
    xc = r_cb_ref[...] + r_cw_ref[nback:nback + 1, :] * xl
    for k in range(1, CONV_WIDTH):
        back = pre - k * SUBLANES
        xc = xc + r_cw_ref[nback - k:nback - k + 1, :] * xl_scr[back:back + tc, :]
    ra = _dot(xc.astype(BF16), r_wax_ref[...]) + r_bax_ref[...]
    r = _sigmoid(ra[:, 0:W])
    ig = _sigmoid(ra[:, W:2 * W])
    log_a = r_clam_ref[...] * r
    a = jnp.exp(log_a)
    mult = jnp.sqrt(-jnp.tanh(log_a) * (a * a + 1.0))
    a_scr[...] = a
    b_scr[...] = mult * (ig * xc)

    f = _dot(x, wf_ref[...]) + bf_ref[...]
    lf = jnp.minimum(f, 0.0) - jnp.log1p(jnp.exp(-jnp.abs(f)))
    hi, mid, lo = _split3(lf)
    tri = tri_ref[...]
    cum = _dot(tri, hi) + _dot(tri, mid) + _dot(tri, lo) + fcarry_scr[...]
    fcarry_scr[...] = cum[tc - 1:tc, :]
    c_hi, c_mid, c_lo = _split3(cum * LOG2E)
    lane = lax.broadcasted_iota(jnp.int32, cum.shape, 1)
    h8 = FOX_HEADS
    c3 = jnp.where(lane < h8, c_hi.astype(F32),
                   jnp.where(lane < 2 * h8, pltpu.roll(c_mid.astype(F32), h8, 1),
                             jnp.where(lane < 3 * h8, pltpu.roll(c_lo.astype(F32), 2 * h8, 1), 0.0)))
    c3 = c3.astype(BF16)
    low = lax.broadcasted_iota(jnp.int32, (tc, LANES), 1) < FOX_HEAD_DIM

    def augment(zz, sel_ref, one_ref, out_ref):
        extra = _dot(c3, sel_ref[...]) + one_ref[...]
        for hp in range(FOX_HEADS // 2):
            two = zz[:, hp * LANES:(hp + 1) * LANES]
            for odd in range(2):
                h = 2 * hp + odd
                src = pltpu.roll(two, FOX_HEAD_DIM, 1) if odd else two
                out_ref[0, :, h * LANES:(h + 1) * LANES] = jnp.where(
                    low, src, extra[:, h * LANES:(h + 1) * LANES]).astype(BF16)

    augment(_dot(x, wqk_ref[:, 0:W]), selq_ref, oneq_ref, qa_ref)
    augment(_dot(x, wqk_ref[:, W:2 * W]), selk_ref, onek_ref, ka_ref)
    nt = (((1,), (1,)), ((), ()))
    vt_ref[0] = lax.dot_general(wvt_ref[...], x, nt, preferred_element_type=F32).astype(BF16)

    _s5_scan(bu_scr, hb_scr, sl_scr, se_scr, scarry_scr, s_lr_ref, s_li_ref, S)
    h = jnp.zeros((SUBLANES, W), F32)
    ap = jnp.ones((SUBLANES, W), F32)
    for t in range(S):
        rows = slice(t * SUBLANES, (t + 1) * SUBLANES)
        at = a_scr[rows, :]
        h = at * h + b_scr[rows, :]
        ap = at * ap
        b_scr[rows, :] = h
        a_scr[rows, :] = ap
    rl_scr[0:SUBLANES, :] = h
    rl_scr[SUBLANES:2 * SUBLANES, :] = ap
    e = rcarry_scr[...]
    for c in range(SUBLANES):
        re_scr[c:c + 1, :] = e
        e = rl_scr[c:c + 1, :] + rl_scr[SUBLANES + c:SUBLANES + c + 1, :] * e
    rcarry_scr[...] = e

    ys = [_dot(hb_scr[:, blk * 2 * H:(blk + 1) * 2 * H], s_wc_ref[blk]) for blk in range(S5_SLABS)]
    y = jnp.concatenate(ys, axis=1) + s_d_ref[...] * u_scr[...]
    y = _gelu(y)
    y = y * _sigmoid(_dot(y.astype(BF16), s_wg_ref[...]) + s_bg_ref[...])
    ys_ref[0] = _dot(permt_ref[...], y.astype(BF16)).astype(BF16)
    et = re_scr[...][None]
    hs = b_scr[...].reshape(S, SUBLANES, W) + a_scr[...].reshape(S, SUBLANES, W) * et
    yr = (_gelu(g_scr[...]) * hs.reshape(tc, W)).astype(BF16)
    yr_ref[0] = _dot(permt_ref[...], yr).astype(BF16)


def _branch_call(xb, l, perm, permt, layered, shared):
    bsz, L, D = xb.shape
    tc = ROW_TILE
    W = BRANCH_WIDTH
    n = FOX_HEADS * LANES
    ncol = 2 * S5_NSTATE
    nback = CONV_WIDTH - 1
    args = (xb, perm, permt) + tuple(layered) + tuple(shared)
    in_specs = ([_row_spec(D), _shared_spec(perm), _shared_spec(permt)]
                + [_layer_spec(a, l) for a in layered] + [_shared_spec(a) for a in shared])
    f32 = lambda *s: pltpu.VMEM(s, F32)
    return pl.pallas_call(
        _branch_kernel,
        grid=(bsz, L // tc),
        in_specs=in_specs,
        out_specs=[_row_spec(W), _row_spec(W), _row_spec(n), _row_spec(n),
                   pl.BlockSpec((1, W, tc), lambda b, i: (b, 0, i))],
        out_shape=[jax.ShapeDtypeStruct((bsz, L, W), BF16), jax.ShapeDtypeStruct((bsz, L, W), BF16),
                   jax.ShapeDtypeStruct((bsz, L, n), BF16), jax.ShapeDtypeStruct((bsz, L, n), BF16),
                   jax.ShapeDtypeStruct((bsz, W, L), BF16)],
        scratch_shapes=[f32(tc, ncol), pltpu.VMEM((tc, ncol), BF16), f32(SUBLANES, ncol),
                        f32(SUBLANES, ncol), f32(1, ncol), f32(tc, W),
                        f32(tc + nback * SUBLANES, W), f32(tc, W), f32(tc, W), f32(tc, W),
                        f32(2 * SUBLANES, W), f32(SUBLANES, W), f32(1, W), f32(nback * SUBLANES, W),
                        f32(1, LANES)],
        compiler_params=_params(2),
        name="branch",
    )(*args)


def _fox_kernel(qa_ref, ka_ref, vt_ref, o_ref, acc_scr, ot_scr, m_scr, st_scr, mx_scr):
    T = qa_ref.shape[1]
    i = pl.program_id(1)
    dh = FOX_HEAD_DIM
    nt = (((1,), (1,)), ((), ()))
    key = lax.broadcasted_iota(jnp.int32, (T, T), 0)
    qry = lax.broadcasted_iota(jnp.int32, (T, T), 1)
    causal = key <= qry
    ones = jnp.ones((FOX_ACC_ROWS - dh, T), BF16)

    m_scr[...] = jnp.full(m_scr.shape, -jnp.inf, F32)
    acc_scr[...] = jnp.zeros(acc_scr.shape, F32)

    def scores(j, slot, masked):
        r0 = pl.multiple_of(j * T, T)
        for h in range(FOX_HEADS):
            hl = slice(h * LANES, (h + 1) * LANES)
            st = lax.dot_general(ka_ref[0, pl.ds(r0, T), hl], qa_ref[0, :, hl], nt,
                                 preferred_element_type=F32)
            if masked:
                st = jnp.where(causal, st, -jnp.inf)
            st_scr[slot, h] = st
            mx_scr[slot, h:h + 1, :] = jnp.max(st, axis=0, keepdims=True)

    def consume(j, slot):
        r0 = pl.multiple_of(j * T, T)
        m_all = m_scr[...]
        mx_all = mx_scr[slot]
        m_rows = []
        for h in range(FOX_HEADS):
            ha = slice(h * FOX_ACC_ROWS, (h + 1) * FOX_ACC_ROWS)
            m = m_all[h:h + 1, :]
            m_new = jnp.maximum(m, mx_all[h:h + 1, :])
            alpha = jnp.exp2(m - m_new)
            pt = jnp.exp2(st_scr[slot, h] - m_new).astype(BF16)
            m_rows.append(m_new)
            v1 = jnp.concatenate([vt_ref[0, h * dh:(h + 1) * dh, pl.ds(r0, T)], ones], axis=0)
            acc_scr[ha, :] = alpha * acc_scr[ha, :] + _dot(v1, pt)
        m_scr[...] = jnp.concatenate(m_rows, axis=0)

    npairs = jnp.maximum(i - 1, 0) // 2
    tail = 2 * npairs

    @pl.when(i > 0)
    def _():
        scores(0, 0, False)

    def body(p, carry):
        j = 2 * p
        scores(j + 1, 1, False)
        consume(j, 0)
        scores(j + 2, 0, False)
        consume(j + 1, 1)
        return carry

    lax.fori_loop(0, npairs, body, 0)

    @pl.when(i == 0)
    def _():
        scores(0, 0, True)
        consume(0, 0)

    @pl.when(i % 2 == 1)
    def _():
        scores(i, 1, True)
        consume(tail, 0)
        consume(i, 1)

    @pl.when(jnp.logical_and(i > 0, i % 2 == 0))
    def _():
        scores(tail + 1, 1, False)
        consume(tail, 0)
        scores(i, 0, True)
        consume(tail + 1, 1)
        consume(i, 0)
    for h in range(FOX_HEADS):
        a0 = h * FOX_ACC_ROWS
        ot_scr[h * dh:(h + 1) * dh, :] = acc_scr[a0:a0 + dh, :] / acc_scr[a0 + dh:a0 + dh + 1, :]
    o_ref[0] = ot_scr[...].T.astype(BF16)


def _fox_call(qa, ka, vt):
    bsz, L, n = qa.shape
    W = BRANCH_WIDTH
    T = ATT_TILE
    return pl.pallas_call(
        _fox_kernel,
        grid=(bsz, L // T),
        in_specs=[_row_spec(n, T),
                  pl.BlockSpec((1, L, n), lambda b, i: (b, 0, 0)),
                  pl.BlockSpec((1, W, L), lambda b, i: (b, 0, 0))],
        out_specs=_row_spec(W, T),
        out_shape=jax.ShapeDtypeStruct((bsz, L, W), BF16),
        scratch_shapes=[pltpu.VMEM((FOX_HEADS * FOX_ACC_ROWS, T), F32), pltpu.VMEM((W, T), F32),
                        pltpu.VMEM((FOX_HEADS, T), F32), pltpu.VMEM((2, FOX_HEADS, T, T), F32),
                        pltpu.VMEM((2, FOX_HEADS, T), F32)],
        compiler_params=_params(2),
        name="fox",
    )(qa, ka, vt)


def _merge_kernel(x_ref, xb_ref, y1_ref, y2_ref, y3_ref, wg_ref, bg_ref, wbr_ref, wo_ref,
                  g_ref, b_ref, o_ref, ob_ref):
    xb = xb_ref[0]
    mixed = None
    for k, y_ref in enumerate((y1_ref, y2_ref, y3_ref)):
        cs = slice(k * D_MODEL, (k + 1) * D_MODEL)
        gate = _sigmoid(_dot(xb, wg_ref[:, cs]) + bg_ref[:, cs])
        term = gate * _dot(y_ref[0], wbr_ref[k])
        mixed = term if mixed is None else mixed + term
    r = ALPHA * x_ref[0] + _dot(mixed.astype(BF16), wo_ref[...])
    y = _layer_norm(r, g_ref[...], b_ref[...])
    o_ref[0] = y
    ob_ref[0] = y.astype(BF16)


def _merge_call(x, xb, y1, y2, y3, l, layered):
    bsz, L, D = x.shape
    W = BRANCH_WIDTH
    return pl.pallas_call(
        _merge_kernel,
        grid=(bsz, L // ROW_TILE),
        in_specs=[_row_spec(D), _row_spec(D), _row_spec(W), _row_spec(W), _row_spec(W)]
        + [_layer_spec(a, l) for a in layered],
        out_specs=[_row_spec(D), _row_spec(D)],
        out_shape=[jax.ShapeDtypeStruct((bsz, L, D), F32), jax.ShapeDtypeStruct((bsz, L, D), BF16)],
        compiler_params=_params(2),
        name="merge",
    )(x, xb, y1, y2, y3, *layered)


def _ffn_kernel(x_ref, xb_ref, wg_ref, wu_ref, wd_ref, g_ref, b_ref, o_ref, ob_ref):
    xb = xb_ref[0]
    acc = None
    for c in range(FFN_HIDDEN // FFN_CHUNK):
        cs = slice(c * FFN_CHUNK, (c + 1) * FFN_CHUNK)
        gt = _dot(xb, wg_ref[:, cs])
        hid = (gt * _sigmoid(gt)) * _dot(xb, wu_ref[:, cs])
        term = _dot(hid.astype(BF16), wd_ref[cs, :])
        acc = term if acc is None else acc + term
    y = _layer_norm(ALPHA * x_ref[0] + acc, g_ref[...], b_ref[...])
    o_ref[0] = y
    ob_ref[0] = y.astype(BF16)


def _ffn_call(x, xb, l, layered):
    bsz, L, D = x.shape
    return pl.pallas_call(
        _ffn_kernel,
        grid=(bsz, L // ROW_TILE),
        in_specs=[_row_spec(D), _row_spec(D)] + [_layer_spec(a, l) for a in layered],
        out_specs=[_row_spec(D), _row_spec(D)],
        out_shape=[jax.ShapeDtypeStruct((bsz, L, D), F32), jax.ShapeDtypeStruct((bsz, L, D), BF16)],
        compiler_params=_params(2),
        name="ffn",
    )(x, xb, *layered)


def _s5_weights(a_re, a_im, log_dt, b_re, b_im, c_re, c_im):
    dt = jnp.exp(log_dt)[:, None]
    mag = jnp.exp(a_re * dt)
    lbr = mag * jnp.cos(a_im * dt)
    lbi = mag * jnp.sin(a_im * dt)
    den = a_re * a_re + a_im * a_im
    kr = ((lbr - 1.0) * a_re + lbi * a_im) / den
    ki = (lbi * a_re - (lbr - 1.0) * a_im) / den
    bbr = kr[:, :, None] * b_re - ki[:, :, None] * b_im
    bbi = kr[:, :, None] * b_im + ki[:, :, None] * b_re
    eye = jnp.eye(S5_SLAB_GROUPS, dtype=F32)

    def in_w(part):
        part = part.reshape(S5_SLABS, S5_SLAB_GROUPS, S5_STATE, S5_GROUP)
        return jnp.einsum('bgpc,gh->bgchp', part, eye).reshape(S5_SLABS, S5_SLAB_IN, S5_HALF)

    def out_w(part):
        part = part.reshape(S5_SLABS, S5_SLAB_GROUPS, S5_GROUP, S5_STATE)
        return jnp.einsum('bgcp,gh->bhpgc', part, eye).reshape(S5_SLABS, S5_HALF, S5_SLAB_IN)

    wb = jnp.concatenate([in_w(bbr), in_w(bbi)], axis=2).astype(BF16)
    wc = jnp.concatenate([out_w(c_re), -out_w(c_im)], axis=1).astype(BF16)
    lr = jnp.broadcast_to(lbr.reshape(1, S5_NSTATE), (SUBLANES, S5_NSTATE))
    li = jnp.broadcast_to(lbi.reshape(1, S5_NSTATE), (SUBLANES, S5_NSTATE))
    return wb, wc, lr, li


def _interleave_perm(tc):
    S = tc // SUBLANES
    p = np.zeros((tc, tc), np.float32)
    for c in range(SUBLANES):
        for t in range(S):
            p[SUBLANES * t + c, c * S + t] = 1.0
    return jnp.asarray(p, BF16), jnp.asarray(p.T, BF16)


def _fox_selectors():
    n = FOX_HEADS * LANES
    selq = np.zeros((LANES, n), np.float32)
    selk = np.zeros((LANES, n), np.float32)
    oneq = np.zeros((1, n), np.float32)
    onek = np.zeros((1, n), np.float32)
    for h in range(FOX_HEADS):
        base = h * LANES + FOX_HEAD_DIM
        for part in range(3):
            selq[part * FOX_HEADS + h, base + part] = 1.0
            selk[part * FOX_HEADS + h, base + 3 + part] = -1.0
            oneq[0, base + 3 + part] = 1.0
            onek[0, base + part] = 1.0
    return (jnp.asarray(selq, BF16), jnp.asarray(selk, BF16), jnp.asarray(oneq), jnp.asarray(onek))


def _block_diag(w):
    n, d, _ = w.shape
    return jnp.einsum('hij,hg->higj', w, jnp.eye(n, dtype=w.dtype)).reshape(n * d, n * d)


def _branch_weights(p):
    W = BRANCH_WIDTH
    nl = p["w_in"].shape[0]
    o_lx, o_q, o_f, o_g = W, 3 * W, 6 * W, 6 * W + FOX_HEADS
    w_in = p["w_in"]
    row = lambda a: a.reshape(nl, 1, -1)
    wb, wc, lr, li = jax.vmap(_s5_weights)(p["s5_a_re"], p["s5_a_im"], p["s5_log_dt"], p["s5_b_re"],
                                           p["s5_b_im"], p["s5_c_re"], p["s5_c_im"])
    w16 = w_in.astype(BF16)
    s5 = (w16[:, :, 0:o_lx], wb, wc, lr, li, row(p["s5_d"]),
          p["s5_w_glu"].astype(BF16), row(p["s5_b_glu"]))
    bd = jax.vmap(_block_diag)
    wax = jnp.concatenate([bd(p["lru_w_a"]), bd(p["lru_w_x"])], axis=2).astype(BF16)
    bax = jnp.concatenate([row(p["lru_b_a"]), row(p["lru_b_x"])], axis=2)
    clam = row(-LRU_C * jax.nn.softplus(-p["lru_lambda"]))
    lru = (w16[:, :, o_lx:o_q], p["lru_conv_w"], row(p["lru_conv_b"]), wax, bax, clam)
    wq = (w_in[:, :, o_q:o_q + W] * (LOG2E * FOX_HEAD_DIM ** -0.5)).astype(BF16)
    w_qk = jnp.concatenate([wq, w16[:, :, o_q + W:o_q + 2 * W]], axis=2)
    w_vt = jnp.transpose(w16[:, :, o_q + 2 * W:o_f], (0, 2, 1))
    w_f = w16[:, :, o_f:o_f + LANES]
    bf = row(jnp.pad(p["b_f"], ((0, 0), (0, LANES - FOX_HEADS))))
    return s5 + lru + (w_qk, w_vt, w_f, bf), w16[:, :, o_g:]


def kernel(x, w_in, b_f, b_gate, s5_a_re, s5_a_im, s5_log_dt, s5_b_re, s5_b_im, s5_c_re, s5_c_im,
           s5_d, s5_w_glu, s5_b_glu, lru_conv_w, lru_conv_b, lru_w_a, lru_b_a, lru_w_x, lru_b_x,
           lru_lambda, w_branch, w_out, ln1_g, ln1_b, w_ffn_gate, w_ffn_up, w_ffn_down, ln2_g, ln2_b):
    p = dict(w_in=w_in, b_f=b_f, s5_a_re=s5_a_re, s5_a_im=s5_a_im, s5_log_dt=s5_log_dt,
             s5_b_re=s5_b_re, s5_b_im=s5_b_im, s5_c_re=s5_c_re, s5_c_im=s5_c_im, s5_d=s5_d,
             s5_w_glu=s5_w_glu, s5_b_glu=s5_b_glu, lru_conv_w=lru_conv_w, lru_conv_b=lru_conv_b,
             lru_w_a=lru_w_a, lru_b_a=lru_b_a, lru_w_x=lru_w_x, lru_b_x=lru_b_x,
             lru_lambda=lru_lambda)
    nl = w_in.shape[0]
    row = lambda a: a.reshape(nl, 1, -1)
    perm, permt = _interleave_perm(ROW_TILE)
    tri = (lax.broadcasted_iota(jnp.int32, (ROW_TILE, ROW_TILE), 1)
           <= lax.broadcasted_iota(jnp.int32, (ROW_TILE, ROW_TILE), 0)).astype(BF16)
    shared = (tri,) + _fox_selectors()
    branch_w, w_gate = _branch_weights(p)
    merge_w = (w_gate, row(b_gate), w_branch.astype(BF16), w_out.astype(BF16), row(ln1_g), row(ln1_b))
    ffn_w = (w_ffn_gate.astype(BF16), w_ffn_up.astype(BF16), w_ffn_down.astype(BF16),
             row(ln2_g), row(ln2_b))
    xb = x.astype(BF16)
    for l in range(nl):
        y_s5, y_lru, qa, ka, vt = _branch_call(xb, l, perm, permt, branch_w, shared)
        y_fox = _fox_call(qa, ka, vt)
        x, xb = _merge_call(x, xb, y_s5, y_lru, y_fox, l, merge_w)
        x, xb = _ffn_call(x, xb, l, ffn_w)
    return x
```

```python
import math

import jax
import jax.numpy as jnp
import numpy as np
from jax import lax
from jax.experimental import pallas as pl
from jax.experimental.pallas import tpu as pltpu

F32 = jnp.float32
BF16 = jnp.bfloat16

D_MODEL = 1024
DEPTH = 4
BRANCH_WIDTH = D_MODEL // 2
N_BRANCH = 3
S5_GROUP = 16
S5_GROUPS = BRANCH_WIDTH // S5_GROUP
S5_STATE = 64
LRU_HEADS = 8
LRU_HEAD_DIM = BRANCH_WIDTH // LRU_HEADS
LRU_C = 8.0
CONV_WIDTH = 4
FOX_HEAD_DIM = 64
FOX_HEADS = BRANCH_WIDTH // FOX_HEAD_DIM
FOX_ACC_ROWS = FOX_HEAD_DIM + 16
LOG2E = math.log2(math.e)
FFN_HIDDEN = ((8 * D_MODEL + 3 * 256 - 1) // (3 * 256)) * 256
ALPHA = (2.0 * DEPTH) ** 0.25
LN_EPS = 1e-5

LANES = 128
SUBLANES = 8
S5_SLABS = 4
S5_SLAB_GROUPS = S5_GROUPS // S5_SLABS
S5_HALF = S5_SLAB_GROUPS * S5_STATE
S5_SLAB_IN = S5_SLAB_GROUPS * S5_GROUP
S5_NSTATE = S5_GROUPS * S5_STATE

ROW_TILE = 512
ATT_TILE = 512
FFN_CHUNK = 1408
VMEM_LIMIT = 56 * 1024 * 1024


def _params(n_axes):
    return pltpu.CompilerParams(
        dimension_semantics=("arbitrary",) * n_axes, vmem_limit_bytes=VMEM_LIMIT)


def _dot(a, b):
    return jnp.dot(a, b, preferred_element_type=F32)


def _sigmoid(x):
    return 1.0 / (1.0 + jnp.exp(-x))


def _gelu(x):
    return 0.5 * x * (1.0 + jnp.tanh(math.sqrt(2.0 / math.pi) * (x + 0.044715 * (x * x * x))))


def _layer_norm(r, g, b):
    mu = jnp.mean(r, axis=-1, keepdims=True)
    d = r - mu
    var = jnp.mean(d * d, axis=-1, keepdims=True)
    return d * lax.rsqrt(var + LN_EPS) * g + b


def _layer_spec(arr, l):
    nd = arr.ndim - 1
    return pl.BlockSpec((None,) + arr.shape[1:], lambda b, i: (l,) + (0,) * nd,
                        pipeline_mode=pl.Buffered(1))


def _shared_spec(arr):
    nd = arr.ndim
    return pl.BlockSpec(arr.shape, lambda b, i: (0,) * nd, pipeline_mode=pl.Buffered(1))


def _row_spec(width, tile=ROW_TILE):
    return pl.BlockSpec((1, tile, width), lambda b, i: (b, i, 0))


def _split3(x):
    hi = x.astype(BF16)
    r1 = x - hi.astype(F32)
    mid = r1.astype(BF16)
    lo = (r1 - mid.astype(F32)).astype(BF16)
    return hi, mid, lo


def _cmul(ar, ai, br, bi):
    return ar * br - ai * bi, ar * bi + ai * br


def _s5_scan(bu_scr, hb_scr, l_scr, e_scr, carry_scr, lr_ref, li_ref, S):
    H = S5_HALF
    nq = H // LANES
    pair = 2
    for b0 in range(0, S5_SLABS, pair):
        cols = []
        for blk in range(b0, b0 + pair):
            for q in range(nq):
                c0 = blk * 2 * H + q * LANES
                l0 = blk * H + q * LANES
                cols.append((slice(c0, c0 + LANES), slice(c0 + H, c0 + H + LANES),
                             slice(l0, l0 + LANES)))
        n = len(cols)
        lr = [lr_ref[:, cl] for _, _, cl in cols]
        li = [li_ref[:, cl] for _, _, cl in cols]
        hr = [jnp.zeros((SUBLANES, LANES), F32)] * n
        hi = [jnp.zeros((SUBLANES, LANES), F32)] * n
        for t in range(S):
            rows = slice(t * SUBLANES, (t + 1) * SUBLANES)
            for k, (cr, ci, _) in enumerate(cols):
                pr, pi = _cmul(lr[k], li[k], hr[k], hi[k])
                hr[k] = pr + bu_scr[rows, cr]
                hi[k] = pi + bu_scr[rows, ci]
                bu_scr[rows, cr] = hr[k]
                bu_scr[rows, ci] = hi[k]
        for k, (cr, ci, _) in enumerate(cols):
            psr, psi = lr[k], li[k]
            for _ in range(S.bit_length() - 1):
                psr, psi = _cmul(psr, psi, psr, psi)
            psr, psi = psr[0:1, :], psi[0:1, :]
            l_scr[:, cr] = hr[k]
            l_scr[:, ci] = hi[k]
            er = carry_scr[:, cr]
            ei = carry_scr[:, ci]
            for c in range(SUBLANES):
                e_scr[c:c + 1, cr] = er
                e_scr[c:c + 1, ci] = ei
                pr, pi = _cmul(psr, psi, er, ei)
                er = l_scr[c:c + 1, cr] + pr
                ei = l_scr[c:c + 1, ci] + pi
            carry_scr[:, cr] = er
            carry_scr[:, ci] = ei
        gr = [e_scr[:, cr] for cr, _, _ in cols]
        gi = [e_scr[:, ci] for _, ci, _ in cols]
        for t2 in range(S // 2):
            rows2 = slice(2 * t2 * SUBLANES, (2 * t2 + 2) * SUBLANES)
            for k, (cr, ci, _) in enumerate(cols):
                outr, outi = [], []
                for t in (2 * t2, 2 * t2 + 1):
                    rows = slice(t * SUBLANES, (t + 1) * SUBLANES)
                    gr[k], gi[k] = _cmul(lr[k], li[k], gr[k], gi[k])
                    outr.append(bu_scr[rows, cr] + gr[k])
                    outi.append(bu_scr[rows, ci] + gi[k])
                hb_scr[rows2, cr] = jnp.concatenate(outr, axis=0).astype(BF16)
                hb_scr[rows2, ci] = jnp.concatenate(outi, axis=0).astype(BF16)


def _branch_kernel(xb_ref, perm_ref, permt_ref,
                   s_win_ref, s_wb_ref, s_wc_ref, s_lr_ref, s_li_ref, s_d_ref, s_wg_ref, s_bg_ref,
                   r_win_ref, r_cw_ref, r_cb_ref, r_wax_ref, r_bax_ref, r_clam_ref,
                   wqk_ref, wvt_ref, wf_ref, bf_ref, tri_ref, selq_ref, selk_ref, oneq_ref, onek_ref,
                   ys_ref, yr_ref, qa_ref, ka_ref, vt_ref,
                   bu_scr, hb_scr, sl_scr, se_scr, scarry_scr, u_scr,
                   xl_scr, a_scr, b_scr, g_scr, rl_scr, re_scr, rcarry_scr, halo_scr, fcarry_scr):
    tc = xb_ref.shape[1]
    S = tc // SUBLANES
    W = BRANCH_WIDTH
    H = S5_HALF
    nback = CONV_WIDTH - 1
    pre = nback * SUBLANES

    @pl.when(pl.program_id(1) == 0)
    def _():
        scarry_scr[...] = jnp.zeros_like(scarry_scr)
        rcarry_scr[...] = jnp.zeros_like(rcarry_scr)
        halo_scr[...] = jnp.zeros_like(halo_scr)
        fcarry_scr[...] = jnp.zeros_like(fcarry_scr)

    x = xb_ref[0]
    xp = _dot(perm_ref[...], x).astype(BF16)

    u = _dot(xp, s_win_ref[...])
    u_scr[...] = u
    ub = u.astype(BF16)
    for blk in range(S5_SLABS):
        bu_scr[:, blk * 2 * H:(blk + 1) * 2 * H] = _dot(
            ub[:, blk * S5_SLAB_IN:(blk + 1) * S5_SLAB_IN], s_wb_ref[blk])

    z = _dot(xp, r_win_ref[...])
    xl = z[:, 0:W]
    g_scr[...] = z[:, W:2 * W]
    xl_scr[pre:pre + tc, :] = xl
    first = lax.broadcasted_iota(jnp.int32, (SUBLANES, W), 0) == 0
    for k in range(1, CONV_WIDTH):
        shifted = pltpu.roll(xl[(S - k) * SUBLANES:(S - k + 1) * SUBLANES, :], 1, 0)
        hrows = slice((k - 1) * SUBLANES, k * SUBLANES)
        xl_scr[(nback - k) * SUBLANES:(nback - k + 1) * SUBLANES, :] = jnp.where(
            first, halo_scr[hrows, :], shifted)
        halo_scr[hrows, :] = shifted
    xc = r_cb_ref[...] + r_cw_ref[nback:nback + 1, :] * xl
    for k in range(1, CONV_WIDTH):
        back = pre - k * SUBLANES
        xc = xc + r_cw_ref[nback - k:nback - k + 1, :] * xl_scr[back:back + tc, :]
    ra = _dot(xc.astype(BF16), r_wax_ref[...]) + r_bax_ref[...]
    r = _sigmoid(ra[:, 0:W])
    ig = _sigmoid(ra[:, W:2 * W])
    log_a = r_clam_ref[...] * r
    a = jnp.exp(log_a)
    mult = jnp.sqrt(-jnp.tanh(log_a) * (a * a + 1.0))
    a_scr[...] = a
    b_scr[...] = mult * (ig * xc)

    f = _dot(x, wf_ref[...]) + bf_ref[...]
    lf = jnp.minimum(f, 0.0) - jnp.log1p(jnp.exp(-jnp.abs(f)))
    hi, mid, lo = _split3(lf)
    tri = tri_ref[...]
    cum = _dot(tri, hi) + _dot(tri, mid) + _dot(tri, lo) + fcarry_scr[...]
    fcarry_scr[...] = cum[tc - 1:tc, :]
    c_hi, c_mid, c_lo = _split3(cum * LOG2E)
    lane = lax.broadcasted_iota(jnp.int32, cum.shape, 1)
    h8 = FOX_HEADS
    c3 = jnp.where(lane < h8, c_hi.astype(F32),
                   jnp.where(lane < 2 * h8, pltpu.roll(c_mid.astype(F32), h8, 1),
                             jnp.where(lane < 3 * h8, pltpu.roll(c_lo.astype(F32), 2 * h8, 1), 0.0)))
    c3 = c3.astype(BF16)
    low = lax.broadcasted_iota(jnp.int32, (tc, LANES), 1) < FOX_HEAD_DIM

    def augment(zz, sel_ref, one_ref, out_ref):
        extra = _dot(c3, sel_ref[...]) + one_ref[...]
        for hp in range(FOX_HEADS // 2):
            two = zz[:, hp * LANES:(hp + 1) * LANES]
            for odd in range(2):
                h = 2 * hp + odd
                src = pltpu.roll(two, FOX_HEAD_DIM, 1) if odd else two
                out_ref[0, :, h * LANES:(h + 1) * LANES] = jnp.where(
                    low, src, extra[:, h * LANES:(h + 1) * LANES]).astype(BF16)

    augment(_dot(x, wqk_ref[:, 0:W]), selq_ref, oneq_ref, qa_ref)
    augment(_dot(x, wqk_ref[:, W:2 * W]), selk_ref, onek_ref, ka_ref)
    nt = (((1,), (1,)), ((), ()))
    vt_ref[0] = lax.dot_general(wvt_ref[...], x, nt, preferred_element_type=F32).astype(BF16)

    _s5_scan(bu_scr, hb_scr, sl_scr, se_scr, scarry_scr, s_lr_ref, s_li_ref, S)
    h = jnp.zeros((SUBLANES, W), F32)
    ap = jnp.ones((SUBLANES, W), F32)
    for t in range(S):
        rows = slice(t * SUBLANES, (t + 1) * SUBLANES)
        at = a_scr[rows, :]
        h = at * h + b_scr[rows, :]
        ap = at * ap
        b_scr[rows, :] = h
        a_scr[rows, :] = ap
    rl_scr[0:SUBLANES, :] = h
    rl_scr[SUBLANES:2 * SUBLANES, :] = ap
    e = rcarry_scr[...]
    for c in range(SUBLANES):
        re_scr[c:c + 1, :] = e
        e = rl_scr[c:c + 1, :] + rl_scr[SUBLANES + c:SUBLANES + c + 1, :] * e
    rcarry_scr[...] = e

    ys = [_dot(hb_scr[:, blk * 2 * H:(blk + 1) * 2 * H], s_wc_ref[blk]) for blk in range(S5_SLABS)]
    y = jnp.concatenate(ys, axis=1) + s_d_ref[...] * u_scr[...]
    y = _gelu(y)
    y = y * _sigmoid(_dot(y.astype(BF16), s_wg_ref[...]) + s_bg_ref[...])
    ys_ref[0] = _dot(permt_ref[...], y.astype(BF16)).astype(BF16)
    et = re_scr[...][None]
    hs = b_scr[...].reshape(S, SUBLANES, W) + a_scr[...].reshape(S, SUBLANES, W) * et
    yr = (_gelu(g_scr[...]) * hs.reshape(tc, W)).astype(BF16)
    yr_ref[0] = _dot(permt_ref[...], yr).astype(BF16)


def _branch_call(xb, l, perm, permt, layered, shared):
    bsz, L, D = xb.shape
    tc = ROW_TILE
    W = BRANCH_WIDTH
    n = FOX_HEADS * LANES
    ncol = 2 * S5_NSTATE
    nback = CONV_WIDTH - 1
    args = (xb, perm, permt) + tuple(layered) + tuple(shared)
    in_specs = ([_row_spec(D), _shared_spec(perm), _shared_spec(permt)]
                + [_layer_spec(a, l) for a in layered] + [_shared_spec(a) for a in shared])
    f32 = lambda *s: pltpu.VMEM(s, F32)
    return pl.pallas_call(
        _branch_kernel,
        grid=(bsz, L // tc),
        in_specs=in_specs,
        out_specs=[_row_spec(W), _row_spec(W), _row_spec(n), _row_spec(n),
                   pl.BlockSpec((1, W, tc), lambda b, i: (b, 0, i))],
        out_shape=[jax.ShapeDtypeStruct((bsz, L, W), BF16), jax.ShapeDtypeStruct((bsz, L, W), BF16),
                   jax.ShapeDtypeStruct((bsz, L, n), BF16), jax.ShapeDtypeStruct((bsz, L, n), BF16),
                   jax.ShapeDtypeStruct((bsz, W, L), BF16)],
        scratch_shapes=[f32(tc, ncol), pltpu.VMEM((tc, ncol), BF16), f32(SUBLANES, ncol),
                        f32(SUBLANES, ncol), f32(1, ncol), f32(tc, W),
                        f32(tc + nback * SUBLANES, W), f32(tc, W), f32(tc, W), f32(tc, W),
                        f32(2 * SUBLANES, W), f32(SUBLANES, W), f32(1, W), f32(nback * SUBLANES, W),
                        f32(1, LANES)],
        compiler_params=_params(2),
        name="branch",
    )(*args)


def _fox_kernel(qa_ref, ka_ref, vt_ref, o_ref, acc_scr, ot_scr, m_scr, st_scr, mx_scr):
    T = qa_ref.shape[1]
    i = pl.program_id(1)
    dh = FOX_HEAD_DIM
    nt = (((1,), (1,)), ((), ()))
    key = lax.broadcasted_iota(jnp.int32, (T, T), 0)
    qry = lax.broadcasted_iota(jnp.int32, (T, T), 1)
    causal = key <= qry
    ones = jnp.ones((FOX_ACC_ROWS - dh, T), BF16)

    m_scr[...] = jnp.full(m_scr.shape, -jnp.inf, F32)
    acc_scr[...] = jnp.zeros(acc_scr.shape, F32)

    def scores(j, slot, masked):
        r0 = pl.multiple_of(j * T, T)
        for h in range(FOX_HEADS):
            hl = slice(h * LANES, (h + 1) * LANES)
            st = lax.dot_general(ka_ref[0, pl.ds(r0, T), hl], qa_ref[0, :, hl], nt,
                                 preferred_element_type=F32)
            if masked:
                st = jnp.where(causal, st, -jnp.inf)
            st_scr[slot, h] = st
            mx_scr[slot, h:h + 1, :] = jnp.max(st, axis=0, keepdims=True)

    def consume(j, slot):
        r0 = pl.multiple_of(j * T, T)
        m_all = m_scr[...]
        mx_all = mx_scr[slot]
        m_rows = []
        for h in range(FOX_HEADS):
            ha = slice(h * FOX_ACC_ROWS, (h + 1) * FOX_ACC_ROWS)
            m = m_all[h:h + 1, :]
            m_new = jnp.maximum(m, mx_all[h:h + 1, :])
            alpha = jnp.exp2(m - m_new)
            pt = jnp.exp2(st_scr[slot, h] - m_new).astype(BF16)
            m_rows.append(m_new)
            v1 = jnp.concatenate([vt_ref[0, h * dh:(h + 1) * dh, pl.ds(r0, T)], ones], axis=0)
            acc_scr[ha, :] = alpha * acc_scr[ha, :] + _dot(v1, pt)
        m_scr[...] = jnp.concatenate(m_rows, axis=0)

    npairs = jnp.maximum(i - 1, 0) // 2
    tail = 2 * npairs

    @pl.when(i > 0)
    def _():
        scores(0, 0, False)

    def body(p, carry):
        j = 2 * p
        scores(j + 1, 1, False)
        consume(j, 0)
        scores(j + 2, 0, False)
        consume(j + 1, 1)
        return carry

    lax.fori_loop(0, npairs, body, 0)

    @pl.when(i == 0)
    def _():
        scores(0, 0, True)
        consume(0, 0)

    @pl.when(i % 2 == 1)
    def _():
        scores(i, 1, True)
        consume(tail, 0)
        consume(i, 1)

    @pl.when(jnp.logical_and(i > 0, i % 2 == 0))
    def _():
        scores(tail + 1, 1, False)
        consume(tail, 0)
        scores(i, 0, True)
        consume(tail + 1, 1)
        consume(i, 0)
    for h in range(FOX_HEADS):
        a0 = h * FOX_ACC_ROWS
        ot_scr[h * dh:(h + 1) * dh, :] = acc_scr[a0:a0 + dh, :] / acc_scr[a0 + dh:a0 + dh + 1, :]
    o_ref[0] = ot_scr[...].T.astype(BF16)


def _fox_call(qa, ka, vt):
    bsz, L, n = qa.shape
    W = BRANCH_WIDTH
    T = ATT_TILE
    return pl.pallas_call(
        _fox_kernel,
        grid=(bsz, L // T),
        in_specs=[_row_spec(n, T),
                  pl.BlockSpec((1, L, n), lambda b, i: (b, 0, 0)),
                  pl.BlockSpec((1, W, L), lambda b, i: (b, 0, 0))],
        out_specs=_row_spec(W, T),
        out_shape=jax.ShapeDtypeStruct((bsz, L, W), BF16),
        scratch_shapes=[pltpu.VMEM((FOX_HEADS * FOX_ACC_ROWS, T), F32), pltpu.VMEM((W, T), F32),
                        pltpu.VMEM((FOX_HEADS, T), F32), pltpu.VMEM((2, FOX_HEADS, T, T), F32),
                        pltpu.VMEM((2, FOX_HEADS, T), F32)],
        compiler_params=_params(2),
        name="fox",
    )(qa, ka, vt)


def _merge_kernel(x_ref, xb_ref, y1_ref, y2_ref, y3_ref, wg_ref, bg_ref, wbr_ref, wo_ref,
                  g_ref, b_ref, o_ref, ob_ref):
    xb = xb_ref[0]
    nt = (((1,), (1,)), ((), ()))
    mixed = None
    for k, y_ref in enumerate((y1_ref, y2_ref, y3_ref)):
        cs = slice(k * D_MODEL, (k + 1) * D_MODEL)
        logits = lax.dot_general(xb, wg_ref[cs, :], nt, preferred_element_type=F32)
        gate = _sigmoid(logits + bg_ref[:, cs])
        term = gate * _dot(y_ref[0], wbr_ref[k])
        mixed = term if mixed is None else mixed + term
    r = ALPHA * x_ref[0] + _dot(mixed.astype(BF16), wo_ref[...])
    y = _layer_norm(r, g_ref[...], b_ref[...])
    o_ref[0] = y
    ob_ref[0] = y.astype(BF16)


def _merge_call(x, xb, y1, y2, y3, l, layered):
    bsz, L, D = x.shape
    W = BRANCH_WIDTH
    return pl.pallas_call(
        _merge_kernel,
        grid=(bsz, L // ROW_TILE),
        in_specs=[_row_spec(D), _row_spec(D), _row_spec(W), _row_spec(W), _row_spec(W)]
        + [_layer_spec(a, l) for a in layered],
        out_specs=[_row_spec(D), _row_spec(D)],
        out_shape=[jax.ShapeDtypeStruct((bsz, L, D), F32), jax.ShapeDtypeStruct((bsz, L, D), BF16)],
        compiler_params=_params(2),
        name="merge",
    )(x, xb, y1, y2, y3, *layered)


def _ffn_kernel(x_ref, xb_ref, wg_ref, wu_ref, wd_ref, g_ref, b_ref, o_ref, ob_ref):
    xb = xb_ref[0]
    acc = None
    for c in range(FFN_HIDDEN // FFN_CHUNK):
        cs = slice(c * FFN_CHUNK, (c + 1) * FFN_CHUNK)
        gt = _dot(xb, wg_ref[:, cs])
        hid = (gt * _sigmoid(gt)) * _dot(xb, wu_ref[:, cs])
        term = _dot(hid.astype(BF16), wd_ref[cs, :])
        acc = term if acc is None else acc + term
    y = _layer_norm(ALPHA * x_ref[0] + acc, g_ref[...], b_ref[...])
    o_ref[0] = y
    ob_ref[0] = y.astype(BF16)


def _ffn_call(x, xb, l, layered):
    bsz, L, D = x.shape
    return pl.pallas_call(
        _ffn_kernel,
        grid=(bsz, L // ROW_TILE),
        in_specs=[_row_spec(D), _row_spec(D)] + [_layer_spec(a, l) for a in layered],
        out_specs=[_row_spec(D), _row_spec(D)],
        out_shape=[jax.ShapeDtypeStruct((bsz, L, D), F32), jax.ShapeDtypeStruct((bsz, L, D), BF16)],
        compiler_params=_params(2),
        name="ffn",
    )(x, xb, *layered)


def _s5_weights(a_re, a_im, log_dt, b_re, b_im, c_re, c_im):
    dt = jnp.exp(log_dt)[:, None]
    mag = jnp.exp(a_re * dt)
    lbr = mag * jnp.cos(a_im * dt)
    lbi = mag * jnp.sin(a_im * dt)
    den = a_re * a_re + a_im * a_im
    kr = ((lbr - 1.0) * a_re + lbi * a_im) / den
    ki = (lbi * a_re - (lbr - 1.0) * a_im) / den
    bbr = kr[:, :, None] * b_re - ki[:, :, None] * b_im
    bbi = kr[:, :, None] * b_im + ki[:, :, None] * b_re
    eye = jnp.eye(S5_SLAB_GROUPS, dtype=F32)

    def in_w(part):
        part = part.reshape(S5_SLABS, S5_SLAB_GROUPS, S5_STATE, S5_GROUP)
        return jnp.einsum('bgpc,gh->bgchp', part, eye).reshape(S5_SLABS, S5_SLAB_IN, S5_HALF)

    def out_w(part):
        part = part.reshape(S5_SLABS, S5_SLAB_GROUPS, S5_GROUP, S5_STATE)
        return jnp.einsum('bgcp,gh->bhpgc', part, eye).reshape(S5_SLABS, S5_HALF, S5_SLAB_IN)

    wb = jnp.concatenate([in_w(bbr), in_w(bbi)], axis=2).astype(BF16)
    wc = jnp.concatenate([out_w(c_re), -out_w(c_im)], axis=1).astype(BF16)
    lr = jnp.broadcast_to(lbr.reshape(1, S5_NSTATE), (SUBLANES, S5_NSTATE))
    li = jnp.broadcast_to(lbi.reshape(1, S5_NSTATE), (SUBLANES, S5_NSTATE))
    return wb, wc, lr, li


def _interleave_perm(tc):
    S = tc // SUBLANES
    p = np.zeros((tc, tc), np.float32)
    for c in range(SUBLANES):
        for t in range(S):
            p[SUBLANES * t + c, c * S + t] = 1.0
    return jnp.asarray(p, BF16), jnp.asarray(p.T, BF16)


def _fox_selectors():
    n = FOX_HEADS * LANES
    selq = np.zeros((LANES, n), np.float32)
    selk = np.zeros((LANES, n), np.float32)
    oneq = np.zeros((1, n), np.float32)
    onek = np.zeros((1, n), np.float32)
    for h in range(FOX_HEADS):
        base = h * LANES + FOX_HEAD_DIM
        for part in range(3):
            selq[part * FOX_HEADS + h, base + part] = 1.0
            selk[part * FOX_HEADS + h, base + 3 + part] = -1.0
            oneq[0, base + 3 + part] = 1.0
            onek[0, base + part] = 1.0
    return (jnp.asarray(selq, BF16), jnp.asarray(selk, BF16), jnp.asarray(oneq), jnp.asarray(onek))


def _block_diag(w):
    n, d, _ = w.shape
    return jnp.einsum('hij,hg->higj', w, jnp.eye(n, dtype=w.dtype)).reshape(n * d, n * d)


def _branch_weights(p):
    W = BRANCH_WIDTH
    nl = p["w_in"].shape[0]
    o_lx, o_q, o_f, o_g = W, 3 * W, 6 * W, 6 * W + FOX_HEADS
    w_in = p["w_in"]
    row = lambda a: a.reshape(nl, 1, -1)
    wb, wc, lr, li = jax.vmap(_s5_weights)(p["s5_a_re"], p["s5_a_im"], p["s5_log_dt"], p["s5_b_re"],
                                           p["s5_b_im"], p["s5_c_re"], p["s5_c_im"])
    w_t = jnp.transpose(w_in, (0, 2, 1))

    def cols(a, b, scale=None):
        part = w_t[:, a:b, :] if scale is None else w_t[:, a:b, :] * scale
        return jnp.transpose(part.astype(BF16), (0, 2, 1))

    s5 = (cols(0, o_lx), wb, wc, lr, li, row(p["s5_d"]),
          p["s5_w_glu"].astype(BF16), row(p["s5_b_glu"]))
    bd = jax.vmap(_block_diag)
    wax = jnp.concatenate([bd(p["lru_w_a"]), bd(p["lru_w_x"])], axis=2).astype(BF16)
    bax = jnp.concatenate([row(p["lru_b_a"]), row(p["lru_b_x"])], axis=2)
    clam = row(-LRU_C * jax.nn.softplus(-p["lru_lambda"]))
    lru = (cols(o_lx, o_q), p["lru_conv_w"], row(p["lru_conv_b"]), wax, bax, clam)
    w_qk = jnp.concatenate([cols(o_q, o_q + W, LOG2E * FOX_HEAD_DIM ** -0.5),
                            cols(o_q + W, o_q + 2 * W)], axis=2)
    w_vt = w_t[:, o_q + 2 * W:o_f, :].astype(BF16)
    w_f = cols(o_f, o_f + LANES)
    bf = row(jnp.pad(p["b_f"], ((0, 0), (0, LANES - FOX_HEADS))))
    return s5 + lru + (w_qk, w_vt, w_f, bf), w_t[:, o_g:, :].astype(BF16)


def kernel(x, w_in, b_f, b_gate, s5_a_re, s5_a_im, s5_log_dt, s5_b_re, s5_b_im, s5_c_re, s5_c_im,
           s5_d, s5_w_glu, s5_b_glu, lru_conv_w, lru_conv_b, lru_w_a, lru_b_a, lru_w_x, lru_b_x,
           lru_lambda, w_branch, w_out, ln1_g, ln1_b, w_ffn_gate, w_ffn_up, w_ffn_down, ln2_g, ln2_b):
    p = dict(w_in=w_in, b_f=b_f, s5_a_re=s5_a_re, s5_a_im=s5_a_im, s5_log_dt=s5_log_dt,
             s5_b_re=s5_b_re, s5_b_im=s5_b_im, s5_c_re=s5_c_re, s5_c_im=s5_c_im, s5_d=s5_d,
             s5_w_glu=s5_w_glu, s5_b_glu=s5_b_glu, lru_conv_w=lru_conv_w, lru_conv_b=lru_conv_b,
             lru_w_a=lru_w_a, lru_b_a=lru_b_a, lru_w_x=lru_w_x, lru_b_x=lru_b_x,
             lru_lambda=lru_lambda)
    nl = w_in.shape[0]
    row = lambda a: a.reshape(nl, 1, -1)
    perm, permt = _interleave_perm(ROW_TILE)
    tri = (lax.broadcasted_iota(jnp.int32, (ROW_TILE, ROW_TILE), 1)
           <= lax.broadcasted_iota(jnp.int32, (ROW_TILE, ROW_TILE), 0)).astype(BF16)
    shared = (tri,) + _fox_selectors()
    branch_w, w_gate = _branch_weights(p)
    merge_w = (w_gate, row(b_gate), w_branch.astype(BF16), w_out.astype(BF16), row(ln1_g), row(ln1_b))
    ffn_w = (w_ffn_gate.astype(BF16), w_ffn_up.astype(BF16), w_ffn_down.astype(BF16),
             row(ln2_g), row(ln2_b))
    xb = x.astype(BF16)
    for l in range(nl):
        y_s5, y_lru, qa, ka, vt = _branch_call(xb, l, perm, permt, branch_w, shared)
        y_fox = _fox_call(qa, ka, vt)
        x, xb = _merge_call(x, xb, y_s5, y_lru, y_fox, l, merge_w)
        x, xb = _ffn_call(x, xb, l, ffn_w)
    return x
```

```python
import math

import jax
import jax.numpy as jnp
import numpy as np
from jax import lax
from jax.experimental import pallas as pl
from jax.experimental.pallas import tpu as pltpu

F32 = jnp.float32
BF16 = jnp.bfloat16

D_MODEL = 1024
DEPTH = 4
BRANCH_WIDTH = D_MODEL // 2
N_BRANCH = 3
S5_GROUP = 16
S5_GROUPS = BRANCH_WIDTH // S5_GROUP
S5_STATE = 64
LRU_HEADS = 8
LRU_HEAD_DIM = BRANCH_WIDTH // LRU_HEADS
LRU_C = 8.0
CONV_WIDTH = 4
FOX_HEAD_DIM = 64
FOX_HEADS = BRANCH_WIDTH // FOX_HEAD_DIM
FOX_ACC_ROWS = FOX_HEAD_DIM + 16
LOG2E = math.log2(math.e)
FFN_HIDDEN = ((8 * D_MODEL + 3 * 256 - 1) // (3 * 256)) * 256
ALPHA = (2.0 * DEPTH) ** 0.25
LN_EPS = 1e-5

LANES = 128
SUBLANES = 8
S5_SLABS = 4
S5_SLAB_GROUPS = S5_GROUPS // S5_SLABS
S5_HALF = S5_SLAB_GROUPS * S5_STATE
S5_SLAB_IN = S5_SLAB_GROUPS * S5_GROUP
S5_NSTATE = S5_GROUPS * S5_STATE

ROW_TILE = 512
ATT_TILE = 512
FFN_CHUNK = 1408
VMEM_LIMIT = 56 * 1024 * 1024


def _params(n_axes):
    return pltpu.CompilerParams(
        dimension_semantics=("arbitrary",) * n_axes, vmem_limit_bytes=VMEM_LIMIT)


def _dot(a, b):
    return jnp.dot(a, b, preferred_element_type=F32)


def _sigmoid(x):
    return 1.0 / (1.0 + jnp.exp(-x))


def _gelu(x):
    return 0.5 * x * (1.0 + jnp.tanh(math.sqrt(2.0 / math.pi) * (x + 0.044715 * (x * x * x))))


def _layer_norm(r, g, b):
    mu = jnp.mean(r, axis=-1, keepdims=True)
    d = r - mu
    var = jnp.mean(d * d, axis=-1, keepdims=True)
    return d * lax.rsqrt(var + LN_EPS) * g + b


def _layer_spec(arr, l):
    nd = arr.ndim - 1
    return pl.BlockSpec((None,) + arr.shape[1:], lambda b, i: (l,) + (0,) * nd,
                        pipeline_mode=pl.Buffered(1))


def _shared_spec(arr):
    nd = arr.ndim
    return pl.BlockSpec(arr.shape, lambda b, i: (0,) * nd, pipeline_mode=pl.Buffered(1))


def _row_spec(width, tile=ROW_TILE):
    return pl.BlockSpec((1, tile, width), lambda b, i: (b, i, 0))


def _split3(x):
    hi = x.astype(BF16)
    r1 = x - hi.astype(F32)
    mid = r1.astype(BF16)
    lo = (r1 - mid.astype(F32)).astype(BF16)
    return hi, mid, lo


def _cmul(ar, ai, br, bi):
    return ar * br - ai * bi, ar * bi + ai * br


def _s5_scan(bu_scr, hb_scr, l_scr, e_scr, carry_scr, lr_ref, li_ref, S):
    H = S5_HALF
    nq = H // LANES
    pair = 2
    for b0 in range(0, S5_SLABS, pair):
        cols = []
        for blk in range(b0, b0 + pair):
            for q in range(nq):
                c0 = blk * 2 * H + q * LANES
                l0 = blk * H + q * LANES
                cols.append((slice(c0, c0 + LANES), slice(c0 + H, c0 + H + LANES),
                             slice(l0, l0 + LANES)))
        n = len(cols)
        lr = [lr_ref[:, cl] for _, _, cl in cols]
        li = [li_ref[:, cl] for _, _, cl in cols]
        hr = [jnp.zeros((SUBLANES, LANES), F32)] * n
        hi = [jnp.zeros((SUBLANES, LANES), F32)] * n
        for t in range(S):
            rows = slice(t * SUBLANES, (t + 1) * SUBLANES)
            for k, (cr, ci, _) in enumerate(cols):
                pr, pi = _cmul(lr[k], li[k], hr[k], hi[k])
                hr[k] = pr + bu_scr[rows, cr]
                hi[k] = pi + bu_scr[rows, ci]
                bu_scr[rows, cr] = hr[k]
                bu_scr[rows, ci] = hi[k]
        for k, (cr, ci, _) in enumerate(cols):
            psr, psi = lr[k], li[k]
            for _ in range(S.bit_length() - 1):
                psr, psi = _cmul(psr, psi, psr, psi)
            psr, psi = psr[0:1, :], psi[0:1, :]
            l_scr[:, cr] = hr[k]
            l_scr[:, ci] = hi[k]
            er = carry_scr[:, cr]
            ei = carry_scr[:, ci]
            for c in range(SUBLANES):
                e_scr[c:c + 1, cr] = er
                e_scr[c:c + 1, ci] = ei
                pr, pi = _cmul(psr, psi, er, ei)
                er = l_scr[c:c + 1, cr] + pr
                ei = l_scr[c:c + 1, ci] + pi
            carry_scr[:, cr] = er
            carry_scr[:, ci] = ei
        gr = [e_scr[:, cr] for cr, _, _ in cols]
        gi = [e_scr[:, ci] for _, ci, _ in cols]
        for t2 in range(S // 2):
            rows2 = slice(2 * t2 * SUBLANES, (2 * t2 + 2) * SUBLANES)
            for k, (cr, ci, _) in enumerate(cols):
                outr, outi = [], []
                for t in (2 * t2, 2 * t2 + 1):
                    rows = slice(t * SUBLANES, (t + 1) * SUBLANES)
                    gr[k], gi[k] = _cmul(lr[k], li[k], gr[k], gi[k])
                    outr.append(bu_scr[rows, cr] + gr[k])
                    outi.append(bu_scr[rows, ci] + gi[k])
                hb_scr[rows2, cr] = jnp.concatenate(outr, axis=0).astype(BF16)
                hb_scr[rows2, ci] = jnp.concatenate(outi, axis=0).astype(BF16)


def _branch_kernel(xb_ref, perm_ref, permt_ref,
                   s_win_ref, s_wb_ref, s_wc_ref, s_lr_ref, s_li_ref, s_d_ref, s_wg_ref, s_bg_ref,
                   r_win_ref, r_cw_ref, r_cb_ref, r_wax_ref, r_bax_ref, r_clam_ref,
                   wqk_ref, wvt_ref, wf_ref, bf_ref, tri_ref, selq_ref, selk_ref, oneq_ref, onek_ref,
                   ys_ref, yr_ref, qa_ref, ka_ref, vt_ref,
                   bu_scr, hb_scr, sl_scr, se_scr, scarry_scr, u_scr,
                   xl_scr, a_scr, b_scr, g_scr, rl_scr, re_scr, rcarry_scr, halo_scr, fcarry_scr):
    tc = xb_ref.shape[1]
    S = tc // SUBLANES
    W = BRANCH_WIDTH
    H = S5_HALF
    nback = CONV_WIDTH - 1
    pre = nback * SUBLANES

    @pl.when(pl.program_id(1) == 0)
    def _():
        scarry_scr[...] = jnp.zeros_like(scarry_scr)
        rcarry_scr[...] = jnp.zeros_like(rcarry_scr)
        halo_scr[...] = jnp.zeros_like(halo_scr)
        fcarry_scr[...] = jnp.zeros_like(fcarry_scr)

    x = xb_ref[0].astype(BF16)
    xp = _dot(perm_ref[...], x).astype(BF16)

    u = _dot(xp, s_win_ref[...])
    u_scr[...] = u
    ub = u.astype(BF16)
    for blk in range(S5_SLABS):
        bu_scr[:, blk * 2 * H:(blk + 1) * 2 * H] = _dot(
            ub[:, blk * S5_SLAB_IN:(blk + 1) * S5_SLAB_IN], s_wb_ref[blk])

    z = _dot(xp, r_win_ref[...])
    xl = z[:, 0:W]
    g_scr[...] = z[:, W:2 * W]
    xl_scr[pre:pre + tc, :] = xl
    first = lax.broadcasted_iota(jnp.int32, (SUBLANES, W), 0) == 0
    for k in range(1, CONV_WIDTH):
        shifted = pltpu.roll(xl[(S - k) * SUBLANES:(S - k + 1) * SUBLANES, :], 1, 0)
        hrows = slice((k - 1) * SUBLANES, k * SUBLANES)
        xl_scr[(nback - k) * SUBLANES:(nback - k + 1) * SUBLANES, :] = jnp.where(
            first, halo_scr[hrows, :], shifted)
        halo_scr[hrows, :] = shifted
    xc = r_cb_ref[...] + r_cw_ref[nback:nback + 1, :] * xl
    for k in range(1, CONV_WIDTH):
        back = pre - k * SUBLANES
        xc = xc + r_cw_ref[nback - k:nback - k + 1, :] * xl_scr[back:back + tc, :]
    ra = _dot(xc.astype(BF16), r_wax_ref[...]) + r_bax_ref[...]
    r = _sigmoid(ra[:, 0:W])
    ig = _sigmoid(ra[:, W:2 * W])
    log_a = r_clam_ref[...] * r
    a = jnp.exp(log_a)
    mult = jnp.sqrt(-jnp.tanh(log_a) * (a * a + 1.0))
    a_scr[...] = a
    b_scr[...] = mult * (ig * xc)

    f = _dot(x, wf_ref[...]) + bf_ref[...]
    lf = jnp.minimum(f, 0.0) - jnp.log1p(jnp.exp(-jnp.abs(f)))
    hi, mid, lo = _split3(lf)
    tri = tri_ref[...]
    cum = _dot(tri, hi) + _dot(tri, mid) + _dot(tri, lo) + fcarry_scr[...]
    fcarry_scr[...] = cum[tc - 1:tc, :]
    c_hi, c_mid, c_lo = _split3(cum * LOG2E)
    lane = lax.broadcasted_iota(jnp.int32, cum.shape, 1)
    h8 = FOX_HEADS
    c3 = jnp.where(lane < h8, c_hi.astype(F32),
                   jnp.where(lane < 2 * h8, pltpu.roll(c_mid.astype(F32), h8, 1),
                             jnp.where(lane < 3 * h8, pltpu.roll(c_lo.astype(F32), 2 * h8, 1), 0.0)))
    c3 = c3.astype(BF16)
    low = lax.broadcasted_iota(jnp.int32, (tc, LANES), 1) < FOX_HEAD_DIM

    def augment(zz, sel_ref, one_ref, out_ref):
        extra = _dot(c3, sel_ref[...]) + one_ref[...]
        for hp in range(FOX_HEADS // 2):
            two = zz[:, hp * LANES:(hp + 1) * LANES]
            for odd in range(2):
                h = 2 * hp + odd
                src = pltpu.roll(two, FOX_HEAD_DIM, 1) if odd else two
                out_ref[0, :, h * LANES:(h + 1) * LANES] = jnp.where(
                    low, src, extra[:, h * LANES:(h + 1) * LANES]).astype(BF16)

    augment(_dot(x, wqk_ref[:, 0:W]), selq_ref, oneq_ref, qa_ref)
    augment(_dot(x, wqk_ref[:, W:2 * W]), selk_ref, onek_ref, ka_ref)
    nt = (((1,), (1,)), ((), ()))
    vt_ref[0] = lax.dot_general(wvt_ref[...], x, nt, preferred_element_type=F32).astype(BF16)

    _s5_scan(bu_scr, hb_scr, sl_scr, se_scr, scarry_scr, s_lr_ref, s_li_ref, S)
    h = jnp.zeros((SUBLANES, W), F32)
    ap = jnp.ones((SUBLANES, W), F32)
    for t in range(S):
        rows = slice(t * SUBLANES, (t + 1) * SUBLANES)
        at = a_scr[rows, :]
        h = at * h + b_scr[rows, :]
        ap = at * ap
        b_scr[rows, :] = h
        a_scr[rows, :] = ap
    rl_scr[0:SUBLANES, :] = h
    rl_scr[SUBLANES:2 * SUBLANES, :] = ap
    e = rcarry_scr[...]
    for c in range(SUBLANES):
        re_scr[c:c + 1, :] = e
        e = rl_scr[c:c + 1, :] + rl_scr[SUBLANES + c:SUBLANES + c + 1, :] * e
    rcarry_scr[...] = e

    ys = [_dot(hb_scr[:, blk * 2 * H:(blk + 1) * 2 * H], s_wc_ref[blk]) for blk in range(S5_SLABS)]
    y = jnp.concatenate(ys, axis=1) + s_d_ref[...] * u_scr[...]
    y = _gelu(y)
    y = y * _sigmoid(_dot(y.astype(BF16), s_wg_ref[...]) + s_bg_ref[...])
    ys_ref[0] = _dot(permt_ref[...], y.astype(BF16)).astype(BF16)
    et = re_scr[...][None]
    hs = b_scr[...].reshape(S, SUBLANES, W) + a_scr[...].reshape(S, SUBLANES, W) * et
    yr = (_gelu(g_scr[...]) * hs.reshape(tc, W)).astype(BF16)
    yr_ref[0] = _dot(permt_ref[...], yr).astype(BF16)


def _branch_call(xb, l, perm, permt, layered, shared):
    bsz, L, D = xb.shape
    tc = ROW_TILE
    W = BRANCH_WIDTH
    n = FOX_HEADS * LANES
    ncol = 2 * S5_NSTATE
    nback = CONV_WIDTH - 1
    args = (xb, perm, permt) + tuple(layered) + tuple(shared)
    in_specs = ([_row_spec(D), _shared_spec(perm), _shared_spec(permt)]
                + [_layer_spec(a, l) for a in layered] + [_shared_spec(a) for a in shared])
    f32 = lambda *s: pltpu.VMEM(s, F32)
    return pl.pallas_call(
        _branch_kernel,
        grid=(bsz, L // tc),
        in_specs=in_specs,
        out_specs=[_row_spec(W), _row_spec(W), _row_spec(n), _row_spec(n),
                   pl.BlockSpec((1, W, tc), lambda b, i: (b, 0, i))],
        out_shape=[jax.ShapeDtypeStruct((bsz, L, W), BF16), jax.ShapeDtypeStruct((bsz, L, W), BF16),
                   jax.ShapeDtypeStruct((bsz, L, n), BF16), jax.ShapeDtypeStruct((bsz, L, n), BF16),
                   jax.ShapeDtypeStruct((bsz, W, L), BF16)],
        scratch_shapes=[f32(tc, ncol), pltpu.VMEM((tc, ncol), BF16), f32(SUBLANES, ncol),
                        f32(SUBLANES, ncol), f32(1, ncol), f32(tc, W),
                        f32(tc + nback * SUBLANES, W), f32(tc, W), f32(tc, W), f32(tc, W),
                        f32(2 * SUBLANES, W), f32(SUBLANES, W), f32(1, W), f32(nback * SUBLANES, W),
                        f32(1, LANES)],
        compiler_params=_params(2),
        name="branch",
    )(*args)


def _fox_kernel(qa_ref, ka_ref, vt_ref, o_ref, acc_scr, ot_scr, m_scr, st_scr, mx_scr):
    T = qa_ref.shape[1]
    i = pl.program_id(1)
    dh = FOX_HEAD_DIM
    nt = (((1,), (1,)), ((), ()))
    key = lax.broadcasted_iota(jnp.int32, (T, T), 0)
    qry = lax.broadcasted_iota(jnp.int32, (T, T), 1)
    causal = key <= qry
    ones = jnp.ones((FOX_ACC_ROWS - dh, T), BF16)

    m_scr[...] = jnp.full(m_scr.shape, -jnp.inf, F32)
    acc_scr[...] = jnp.zeros(acc_scr.shape, F32)

    def scores(j, slot, masked):
        r0 = pl.multiple_of(j * T, T)
        for h in range(FOX_HEADS):
            hl = slice(h * LANES, (h + 1) * LANES)
            st = lax.dot_general(ka_ref[0, pl.ds(r0, T), hl], qa_ref[0, :, hl], nt,
                                 preferred_element_type=F32)
            if masked:
                st = jnp.where(causal, st, -jnp.inf)
            st_scr[slot, h] = st
            mx_scr[slot, h:h + 1, :] = jnp.max(st, axis=0, keepdims=True)

    def consume(j, slot):
        r0 = pl.multiple_of(j * T, T)
        m_all = m_scr[...]
        mx_all = mx_scr[slot]
        m_rows = []
        for h in range(FOX_HEADS):
            ha = slice(h * FOX_ACC_ROWS, (h + 1) * FOX_ACC_ROWS)
            m = m_all[h:h + 1, :]
            m_new = jnp.maximum(m, mx_all[h:h + 1, :])
            alpha = jnp.exp2(m - m_new)
            pt = jnp.exp2(st_scr[slot, h] - m_new).astype(BF16)
            m_rows.append(m_new)
            v1 = jnp.concatenate([vt_ref[0, h * dh:(h + 1) * dh, pl.ds(r0, T)], ones], axis=0)
            acc_scr[ha, :] = alpha * acc_scr[ha, :] + _dot(v1, pt)
        m_scr[...] = jnp.concatenate(m_rows, axis=0)

    npairs = jnp.maximum(i - 1, 0) // 2
    tail = 2 * npairs

    @pl.when(i > 0)
    def _():
        scores(0, 0, False)

    def body(p, carry):
        j = 2 * p
        scores(j + 1, 1, False)
        consume(j, 0)
        scores(j + 2, 0, False)
        consume(j + 1, 1)
        return carry

    lax.fori_loop(0, npairs, body, 0)

    @pl.when(i == 0)
    def _():
        scores(0, 0, True)
        consume(0, 0)

    @pl.when(i % 2 == 1)
    def _():
        scores(i, 1, True)
        consume(tail, 0)
        consume(i, 1)

    @pl.when(jnp.logical_and(i > 0, i % 2 == 0))
    def _():
        scores(tail + 1, 1, False)
        consume(tail, 0)
        scores(i, 0, True)
        consume(tail + 1, 1)
        consume(i, 0)
    for h in range(FOX_HEADS):
        a0 = h * FOX_ACC_ROWS
        ot_scr[h * dh:(h + 1) * dh, :] = acc_scr[a0:a0 + dh, :] / acc_scr[a0 + dh:a0 + dh + 1, :]
    o_ref[0] = ot_scr[...].T.astype(BF16)


def _fox_call(qa, ka, vt):
    bsz, L, n = qa.shape
    W = BRANCH_WIDTH
    T = ATT_TILE
    return pl.pallas_call(
        _fox_kernel,
        grid=(bsz, L // T),
        in_specs=[_row_spec(n, T),
                  pl.BlockSpec((1, L, n), lambda b, i: (b, 0, 0)),
                  pl.BlockSpec((1, W, L), lambda b, i: (b, 0, 0))],
        out_specs=_row_spec(W, T),
        out_shape=jax.ShapeDtypeStruct((bsz, L, W), BF16),
        scratch_shapes=[pltpu.VMEM((FOX_HEADS * FOX_ACC_ROWS, T), F32), pltpu.VMEM((W, T), F32),
                        pltpu.VMEM((FOX_HEADS, T), F32), pltpu.VMEM((2, FOX_HEADS, T, T), F32),
                        pltpu.VMEM((2, FOX_HEADS, T), F32)],
        compiler_params=_params(2),
        name="fox",
    )(qa, ka, vt)


def _merge_kernel(x_ref, xb_ref, y1_ref, y2_ref, y3_ref, wg_ref, bg_ref, wbr_ref, wo_ref,
                  g_ref, b_ref, o_ref, ob_ref):
    xb = xb_ref[0].astype(BF16)
    nt = (((1,), (1,)), ((), ()))
    mixed = None
    for k, y_ref in enumerate((y1_ref, y2_ref, y3_ref)):
        cs = slice(k * D_MODEL, (k + 1) * D_MODEL)
        logits = lax.dot_general(xb, wg_ref[cs, :], nt, preferred_element_type=F32)
        gate = _sigmoid(logits + bg_ref[:, cs])
        term = gate * _dot(y_ref[0], wbr_ref[k])
        mixed = term if mixed is None else mixed + term
    r = ALPHA * x_ref[0] + _dot(mixed.astype(BF16), wo_ref[...])
    y = _layer_norm(r, g_ref[...], b_ref[...])
    o_ref[0] = y
    ob_ref[0] = y.astype(BF16)


def _merge_call(x, xb, y1, y2, y3, l, layered):
    bsz, L, D = x.shape
    W = BRANCH_WIDTH
    return pl.pallas_call(
        _merge_kernel,
        grid=(bsz, L // ROW_TILE),
        in_specs=[_row_spec(D), _row_spec(D), _row_spec(W), _row_spec(W), _row_spec(W)]
        + [_layer_spec(a, l) for a in layered],
        out_specs=[_row_spec(D), _row_spec(D)],
        out_shape=[jax.ShapeDtypeStruct((bsz, L, D), F32), jax.ShapeDtypeStruct((bsz, L, D), BF16)],
        compiler_params=_params(2),
        name="merge",
    )(x, xb, y1, y2, y3, *layered)


def _ffn_kernel(x_ref, xb_ref, wg_ref, wu_ref, wd_ref, g_ref, b_ref, o_ref, ob_ref):
    xb = xb_ref[0]
    acc = None
    for c in range(FFN_HIDDEN // FFN_CHUNK):
        cs = slice(c * FFN_CHUNK, (c + 1) * FFN_CHUNK)
        gt = _dot(xb, wg_ref[:, cs])
        hid = (gt * _sigmoid(gt)) * _dot(xb, wu_ref[:, cs])
        term = _dot(hid.astype(BF16), wd_ref[cs, :])
        acc = term if acc is None else acc + term
    y = _layer_norm(ALPHA * x_ref[0] + acc, g_ref[...], b_ref[...])
    o_ref[0] = y
    ob_ref[0] = y.astype(BF16)


def _ffn_call(x, xb, l, layered):
    bsz, L, D = x.shape
    return pl.pallas_call(
        _ffn_kernel,
        grid=(bsz, L // ROW_TILE),
        in_specs=[_row_spec(D), _row_spec(D)] + [_layer_spec(a, l) for a in layered],
        out_specs=[_row_spec(D), _row_spec(D)],
        out_shape=[jax.ShapeDtypeStruct((bsz, L, D), F32), jax.ShapeDtypeStruct((bsz, L, D), BF16)],
        compiler_params=_params(2),
        name="ffn",
    )(x, xb, *layered)


def _s5_weights(a_re, a_im, log_dt, b_re, b_im, c_re, c_im):
    dt = jnp.exp(log_dt)[:, None]
    mag = jnp.exp(a_re * dt)
    lbr = mag * jnp.cos(a_im * dt)
    lbi = mag * jnp.sin(a_im * dt)
    den = a_re * a_re + a_im * a_im
    kr = ((lbr - 1.0) * a_re + lbi * a_im) / den
    ki = (lbi * a_re - (lbr - 1.0) * a_im) / den
    bbr = kr[:, :, None] * b_re - ki[:, :, None] * b_im
    bbi = kr[:, :, None] * b_im + ki[:, :, None] * b_re
    eye = jnp.eye(S5_SLAB_GROUPS, dtype=F32)

    def in_w(part):
        part = part.reshape(S5_SLABS, S5_SLAB_GROUPS, S5_STATE, S5_GROUP)
        return jnp.einsum('bgpc,gh->bgchp', part, eye).reshape(S5_SLABS, S5_SLAB_IN, S5_HALF)

    def out_w(part):
        part = part.reshape(S5_SLABS, S5_SLAB_GROUPS, S5_GROUP, S5_STATE)
        return jnp.einsum('bgcp,gh->bhpgc', part, eye).reshape(S5_SLABS, S5_HALF, S5_SLAB_IN)

    wb = jnp.concatenate([in_w(bbr), in_w(bbi)], axis=2).astype(BF16)
    wc = jnp.concatenate([out_w(c_re), -out_w(c_im)], axis=1).astype(BF16)
    lr = jnp.broadcast_to(lbr.reshape(1, S5_NSTATE), (SUBLANES, S5_NSTATE))
    li = jnp.broadcast_to(lbi.reshape(1, S5_NSTATE), (SUBLANES, S5_NSTATE))
    return wb, wc, lr, li


def _interleave_perm(tc):
    S = tc // SUBLANES
    p = np.zeros((tc, tc), np.float32)
    for c in range(SUBLANES):
        for t in range(S):
            p[SUBLANES * t + c, c * S + t] = 1.0
    return jnp.asarray(p, BF16), jnp.asarray(p.T, BF16)


def _fox_selectors():
    n = FOX_HEADS * LANES
    selq = np.zeros((LANES, n), np.float32)
    selk = np.zeros((LANES, n), np.float32)
    oneq = np.zeros((1, n), np.float32)
    onek = np.zeros((1, n), np.float32)
    for h in range(FOX_HEADS):
        base = h * LANES + FOX_HEAD_DIM
        for part in range(3):
            selq[part * FOX_HEADS + h, base + part] = 1.0
            selk[part * FOX_HEADS + h, base + 3 + part] = -1.0
            oneq[0, base + 3 + part] = 1.0
            onek[0, base + part] = 1.0
    return (jnp.asarray(selq, BF16), jnp.asarray(selk, BF16), jnp.asarray(oneq), jnp.asarray(onek))


def _block_diag(w):
    n, d, _ = w.shape
    return jnp.einsum('hij,hg->higj', w, jnp.eye(n, dtype=w.dtype)).reshape(n * d, n * d)


def _branch_weights(p):
    W = BRANCH_WIDTH
    nl = p["w_in"].shape[0]
    o_lx, o_q, o_f, o_g = W, 3 * W, 6 * W, 6 * W + FOX_HEADS
    w_in = p["w_in"]
    row = lambda a: a.reshape(nl, 1, -1)
    wb, wc, lr, li = jax.vmap(_s5_weights)(p["s5_a_re"], p["s5_a_im"], p["s5_log_dt"], p["s5_b_re"],
                                           p["s5_b_im"], p["s5_c_re"], p["s5_c_im"])
    w_t = jnp.transpose(w_in, (0, 2, 1))

    def cols(a, b, scale=None):
        part = w_t[:, a:b, :] if scale is None else w_t[:, a:b, :] * scale
        return jnp.transpose(part.astype(BF16), (0, 2, 1))

    s5 = (cols(0, o_lx), wb, wc, lr, li, row(p["s5_d"]),
          p["s5_w_glu"].astype(BF16), row(p["s5_b_glu"]))
    bd = jax.vmap(_block_diag)
    wax = jnp.concatenate([bd(p["lru_w_a"]), bd(p["lru_w_x"])], axis=2).astype(BF16)
    bax = jnp.concatenate([row(p["lru_b_a"]), row(p["lru_b_x"])], axis=2)
    clam = row(-LRU_C * jax.nn.softplus(-p["lru_lambda"]))
    lru = (cols(o_lx, o_q), p["lru_conv_w"], row(p["lru_conv_b"]), wax, bax, clam)
    w_qk = jnp.concatenate([cols(o_q, o_q + W, LOG2E * FOX_HEAD_DIM ** -0.5),
                            cols(o_q + W, o_q + 2 * W)], axis=2)
    w_vt = w_t[:, o_q + 2 * W:o_f, :].astype(BF16)
    w_f = cols(o_f, o_f + LANES)
    bf = row(jnp.pad(p["b_f"], ((0, 0), (0, LANES - FOX_HEADS))))
    return s5 + lru + (w_qk, w_vt, w_f, bf), w_t[:, o_g:, :].astype(BF16)


def kernel(x, w_in, b_f, b_gate, s5_a_re, s5_a_im, s5_log_dt, s5_b_re, s5_b_im, s5_c_re, s5_c_im,
           s5_d, s5_w_glu, s5_b_glu, lru_conv_w, lru_conv_b, lru_w_a, lru_b_a, lru_w_x, lru_b_x,
           lru_lambda, w_branch, w_out, ln1_g, ln1_b, w_ffn_gate, w_ffn_up, w_ffn_down, ln2_g, ln2_b):
    p = dict(w_in=w_in, b_f=b_f, s5_a_re=s5_a_re, s5_a_im=s5_a_im, s5_log_dt=s5_log_dt,
             s5_b_re=s5_b_re, s5_b_im=s5_b_im, s5_c_re=s5_c_re, s5_c_im=s5_c_im, s5_d=s5_d,
             s5_w_glu=s5_w_glu, s5_b_glu=s5_b_glu, lru_conv_w=lru_conv_w, lru_conv_b=lru_conv_b,
             lru_w_a=lru_w_a, lru_b_a=lru_b_a, lru_w_x=lru_w_x, lru_b_x=lru_b_x,
             lru_lambda=lru_lambda)
    nl = w_in.shape[0]
    row = lambda a: a.reshape(nl, 1, -1)
    perm, permt = _interleave_perm(ROW_TILE)
    tri = (lax.broadcasted_iota(jnp.int32, (ROW_TILE, ROW_TILE), 1)
           <= lax.broadcasted_iota(jnp.int32, (ROW_TILE, ROW_TILE), 0)).astype(BF16)
    shared = (tri,) + _fox_selectors()
    branch_w, w_gate = _branch_weights(p)
    merge_w = (w_gate, row(b_gate), w_branch.astype(BF16), w_out.astype(BF16), row(ln1_g), row(ln1_b))
    ffn_w = (w_ffn_gate.astype(BF16), w_ffn_up.astype(BF16), w_ffn_down.astype(BF16),
             row(ln2_g), row(ln2_b))
    xb = x
    for l in range(nl):
        y_s5, y_lru, qa, ka, vt = _branch_call(xb, l, perm, permt, branch_w, shared)
        y_fox = _fox_call(qa, ka, vt)
        x, xb = _merge_call(x, xb, y_s5, y_lru, y_fox, l, merge_w)
        x, xb = _ffn_call(x, xb, l, ffn_w)
    return x
```
